```python
import math
import jax, jax.numpy as jnp
from jax import lax
import numpy as np


D_MODEL = 2048
BATCH = 2
SEQ = 16384
DEPTH = 2
DEC_BATCH = 4
DEC_SEQ = 2048
PAST_LEN = 128

HEAD_DIM = 128
N_ATTN_HEADS = 8
ATTN_WIDTH = N_ATTN_HEADS * HEAD_DIM
LRU_WIDTH = D_MODEL - ATTN_WIDTH
N_LRU_BLOCKS = 8
LRU_BLOCK = LRU_WIDTH // N_LRU_BLOCKS
CONV_WIDTH = 4
CONV_LEFT = 2
LRU_C = 8.0
DILATED_CONFIGS = ((128, 1), (512, 4), (2048, 16))
ROPE_THETA = 10000.0
N_KEYS = 128
N_EXPERTS = N_KEYS * N_KEYS
PEER_HEADS = 8
PEER_TOPK = 16
PEER_QDIM = 256
PEER_HALF = PEER_QDIM // 2
PEER_CHUNK = 128
IN_WIDTH = 3 * ATTN_WIDTH + 2 * LRU_WIDTH
EPS = 1e-6
NEG = -1e30
F32 = jnp.float32

kernel_name = 'hymba_dilated_rglru_peer_encoder'


def _rmsnorm(t, g):
    tf = t.astype(F32)
    y = tf * lax.rsqrt(jnp.mean(tf * tf, axis=-1, keepdims=True) + EPS) * g.astype(F32)
    return y.astype(t.dtype)


def _rope(t):
    s, c = t.shape[1], t.shape[3]
    inv = ROPE_THETA ** (-jnp.arange(0, c, 2, dtype=F32) / c)
    ang = jnp.arange(s, dtype=F32)[:, None] * inv[None, :]
    cos = jnp.cos(ang)[None, :, None, :]
    sin = jnp.sin(ang)[None, :, None, :]
    tf = t.astype(F32)
    t1, t2 = tf[..., : c // 2], tf[..., c // 2:]
    return jnp.concatenate([t1 * cos - t2 * sin, t2 * cos + t1 * sin], axis=-1).astype(t.dtype)


def _dilated_branch(q, k, v, window, dilation):
    b, s, h, c = q.shape
    half = window // (2 * dilation)
    blk = half
    L = s // dilation
    nb = -(-L // blk)
    lp = nb * blk

    def strided(t):
        return t.reshape(b, L, dilation, h, c).transpose(0, 2, 3, 1, 4)

    qs, ks, vs = strided(q), strided(k), strided(v)
    qb = jnp.pad(qs, ((0, 0), (0, 0), (0, 0), (0, lp - L), (0, 0))).reshape(b, dilation, h, nb, blk, c)

    def banded(t):
        tp = jnp.pad(t, ((0, 0), (0, 0), (0, 0), (blk, lp - L + blk), (0, 0)))
        return jnp.concatenate(
            [tp[:, :, :, j * blk: j * blk + lp].reshape(b, dilation, h, nb, blk, c) for j in range(3)], axis=-2)

    kb, vb = banded(ks), banded(vs)
    scores = jnp.einsum('brhnqc,brhnkc->brhnqk', qb, kb, preferred_element_type=F32) * (c ** -0.5)
    qi = jnp.arange(lp).reshape(nb, blk, 1)
    kj = jnp.arange(nb)[:, None, None] * blk - blk + jnp.arange(3 * blk)[None, None, :]
    mask = (jnp.abs(qi - kj) <= half) & (kj >= 0) & (kj < L)
    scores = jnp.where(mask, scores, NEG)
    lse = jax.nn.logsumexp(scores, axis=-1)
    p = jnp.exp(scores - lse[..., None])
    o = jnp.einsum('brhnqk,brhnkc->brhnqc', p.astype(v.dtype), vb)
    o = o.reshape(b, dilation, h, lp, c)[:, :, :, :L].transpose(0, 3, 1, 2, 4).reshape(b, s, h, c)
    lse = lse.reshape(b, dilation, h, lp)[:, :, :, :L].transpose(0, 3, 1, 2).reshape(b, s, h)
    return o, lse


def _dilated_attention(q, k, v):
    branches = [_dilated_branch(q, k, v, w, d) for (w, d) in DILATED_CONFIGS]
    outs = jnp.stack([o for (o, _) in branches], axis=0)
    wts = jax.nn.softmax(jnp.stack([l for (_, l) in branches], axis=0), axis=0)
    return jnp.einsum('gbsh,gbshc->bshc', wts.astype(q.dtype), outs)


def _centred_conv(t, w, bias):
    s = t.shape[1]
    tp = jnp.pad(t, ((0, 0), (CONV_LEFT, CONV_WIDTH - 1 - CONV_LEFT), (0, 0)))
    out = bias + tp[:, 0:s] * w[0]
    for j in range(1, CONV_WIDTH):
        out = out + tp[:, j:j + s] * w[j]
    return out


def _rglru(xc, wa, ba, wx, bx, lam, reverse):
    b, s, w = xc.shape
    xg = xc.reshape(b, s, N_LRU_BLOCKS, LRU_BLOCK)
    r = jax.nn.sigmoid((jnp.einsum('bsgi,gij->bsgj', xg, wa).reshape(b, s, w) + ba).astype(F32))
    i = jax.nn.sigmoid((jnp.einsum('bsgi,gij->bsgj', xg, wx).reshape(b, s, w) + bx).astype(F32))
    log_a = -LRU_C * jax.nn.softplus(-lam.astype(F32)) * r
    a = jnp.exp(log_a)
    bt = jnp.sqrt(-jnp.expm1(2.0 * log_a)) * i * xc.astype(F32)

    def combine(left, right):
        a1, b1 = left
        a2, b2 = right
        return a1 * a2, a2 * b1 + b2

    _, hs = lax.associative_scan(combine, (a, bt), reverse=reverse, axis=1)
    return hs


def _peer(xn, wq, keys, u, v):
    t, d = xn.shape
    xs = xn.reshape(t // PEER_CHUNK, PEER_CHUNK, d)

    def chunk(xc):
        q = (xc @ wq).reshape(PEER_CHUNK, PEER_HEADS, 2, PEER_HALF)
        sc = jnp.einsum('thpc,hpkc->thpk', q, keys, preferred_element_type=F32)
        s1, i1 = lax.top_k(sc[:, :, 0], PEER_TOPK)
        s2, i2 = lax.top_k(sc[:, :, 1], PEER_TOPK)
        cand = (s1[..., :, None] + s2[..., None, :]).reshape(PEER_CHUNK, PEER_HEADS, PEER_TOPK * PEER_TOPK)
        top, idx = lax.top_k(cand, PEER_TOPK)
        e = (jnp.take_along_axis(i1, idx // PEER_TOPK, axis=-1) * N_KEYS
             + jnp.take_along_axis(i2, idx % PEER_TOPK, axis=-1))
        g = jax.nn.softmax(top, axis=-1)
        hid = jax.nn.gelu(jnp.einsum('thkd,td->thk', u[e], xc, preferred_element_type=F32), approximate=False)
        return jnp.einsum('thk,thkd->td', (g * hid).astype(xc.dtype), v[e])

    return lax.map(chunk, xs).reshape(t, d)


def _layer(x, ln1_g, w_in, q_norm_g, k_norm_g, conv_w, conv_b, lru_wa, lru_ba, lru_wx, lru_bx,
           lru_lam, attn_out_g, lru_out_g, w_out, ln2_g, peer_wq, peer_keys, peer_u, peer_v):
    b, s, _ = x.shape
    hn = _rmsnorm(x, ln1_g)
    proj = hn @ w_in
    q, k, vv, xr, gate = jnp.split(
        proj, [ATTN_WIDTH, 2 * ATTN_WIDTH, 3 * ATTN_WIDTH, 3 * ATTN_WIDTH + LRU_WIDTH], axis=-1)
    q = _rope(_rmsnorm(q.reshape(b, s, N_ATTN_HEADS, HEAD_DIM), q_norm_g))
    k = _rope(_rmsnorm(k.reshape(b, s, N_ATTN_HEADS, HEAD_DIM), k_norm_g))
    vv = vv.reshape(b, s, N_ATTN_HEADS, HEAD_DIM)
    attn = _dilated_attention(q, k, vv).reshape(b, s, ATTN_WIDTH)
    xc = _centred_conv(xr, conv_w, conv_b)
    hs = (_rglru(xc, lru_wa[0], lru_ba[0], lru_wx[0], lru_bx[0], lru_lam[0], False)
          + _rglru(xc, lru_wa[1], lru_ba[1], lru_wx[1], lru_bx[1], lru_lam[1], True))
    lru = (hs * jax.nn.gelu(gate.astype(F32), approximate=False)).astype(x.dtype)
    mix = jnp.concatenate([_rmsnorm(attn, attn_out_g), _rmsnorm(lru, lru_out_g)], axis=-1) @ w_out
    x = x + mix
    y = _peer(_rmsnorm(x, ln2_g).reshape(b * s, D_MODEL), peer_wq, peer_keys, peer_u, peer_v)
    return x + y.reshape(b, s, D_MODEL)


def setup_inputs(seed: int = 0) -> dict:
    key = jax.random.key(seed)
    ks = jax.random.split(key, 24)
    nrm = lambda k, shape, scale: jax.random.normal(k, shape, F32) * scale
    a_c = jax.random.uniform(ks[12], (DEPTH, 2, LRU_WIDTH), F32, 0.9, 0.999)
    a0 = a_c ** (1.0 / LRU_C)
    lam = jnp.log(a0) - jnp.log1p(-a0)
    return {
        'x_prompt': nrm(ks[0], (BATCH, SEQ, D_MODEL), 1.0),
        'x_sample': nrm(ks[1], (DEC_BATCH, DEC_SEQ, D_MODEL), 1.0),
        'ln1_g': 1.0 + nrm(ks[2], (DEPTH, D_MODEL), 0.02),
        'w_in': nrm(ks[3], (DEPTH, D_MODEL, IN_WIDTH), D_MODEL ** -0.5),
        'q_norm_g': 1.0 + nrm(ks[4], (DEPTH, HEAD_DIM), 0.02),
        'k_norm_g': 1.0 + nrm(ks[5], (DEPTH, HEAD_DIM), 0.02),
        'conv_w': nrm(ks[6], (DEPTH, CONV_WIDTH, LRU_WIDTH), CONV_WIDTH ** -0.5),
        'conv_b': nrm(ks[7], (DEPTH, LRU_WIDTH), 0.01),
        'lru_wa': nrm(ks[8], (DEPTH, 2, N_LRU_BLOCKS, LRU_BLOCK, LRU_BLOCK), LRU_BLOCK ** -0.5),
        'lru_ba': nrm(ks[9], (DEPTH, 2, LRU_WIDTH), 0.01),
        'lru_wx': nrm(ks[10], (DEPTH, 2, N_LRU_BLOCKS, LRU_BLOCK, LRU_BLOCK), LRU_BLOCK ** -0.5),
        'lru_bx': nrm(ks[11], (DEPTH, 2, LRU_WIDTH), 0.01),
        'lru_lam': lam,
        'attn_out_g': 1.0 + nrm(ks[13], (DEPTH, ATTN_WIDTH), 0.02),
        'lru_out_g': 1.0 + nrm(ks[14], (DEPTH, LRU_WIDTH), 0.02),
        'w_out': nrm(ks[15], (DEPTH, D_MODEL, D_MODEL), D_MODEL ** -0.5),
        'ln2_g': 1.0 + nrm(ks[16], (DEPTH, D_MODEL), 0.02),
        'peer_wq': nrm(ks[17], (DEPTH, D_MODEL, PEER_HEADS * PEER_QDIM), D_MODEL ** -0.5),
        'peer_keys': nrm(ks[18], (DEPTH, PEER_HEADS, 2, N_KEYS, PEER_HALF), PEER_HALF ** -0.5),
        'peer_u': nrm(ks[19], (DEPTH, N_EXPERTS, D_MODEL), D_MODEL ** -0.5),
        'peer_v': nrm(ks[20], (DEPTH, N_EXPERTS, D_MODEL), (PEER_HEADS * PEER_TOPK) ** -0.5),
    }


def reference(x_prompt, x_sample, ln1_g, w_in, q_norm_g, k_norm_g, conv_w, conv_b, lru_wa, lru_ba,
              lru_wx, lru_bx, lru_lam, attn_out_g, lru_out_g, w_out, ln2_g, peer_wq, peer_keys,
              peer_u, peer_v):
    params = (ln1_g, w_in, q_norm_g, k_norm_g, conv_w, conv_b, lru_wa, lru_ba, lru_wx, lru_bx,
              lru_lam, attn_out_g, lru_out_g, w_out, ln2_g, peer_wq, peer_keys, peer_u, peer_v)

    def trunk(x):
        for l in range(DEPTH):
            x = _layer(x, *[p[l] for p in params])
        return x

    y_prompt = trunk(x_prompt)
    y_sample = trunk(x_sample)
    return (y_prompt, y_sample)
```

```python
import functools
import math

import jax
import jax.numpy as jnp
from jax import lax
from jax.experimental import pallas as pl
from jax.experimental.pallas import tpu as pltpu

F32 = jnp.float32
BF16 = jnp.bfloat16
I32 = jnp.int32

D_MODEL = 2048
HEAD_DIM = 128
N_HEADS = 8
ATTN_WIDTH = N_HEADS * HEAD_DIM
LRU_WIDTH = D_MODEL - ATTN_WIDTH
N_LRU_BLOCKS = 8
LRU_BLOCK = LRU_WIDTH // N_LRU_BLOCKS
CONV_WIDTH = 4
CONV_LEFT = 2
LRU_C = 8.0
DILATIONS = (1, 4, 16)
HALF_WINDOW = 64
ROPE_THETA = 10000.0
N_KEYS = 128
N_EXPERTS = N_KEYS * N_KEYS
PEER_HEADS = 8
PEER_TOPK = 16
PEER_HALF = 128
N_TERMS = PEER_HEADS * PEER_TOPK
EPS = 1e-6
NEG = -1e30

LANES = 128
SUBLANES = 8
VMEM_LIMIT_BYTES = 56 * 1024 * 1024

TM_IN = 512
N_IN_BLOCKS = 5
TP_ATTN = 2048
TQ = 128
TS_LRU = 512
TM_OUT = 256
TM_ROUTE = 256
TM_PEER = 512
EB_PEER = 512
N_I1_CHUNKS = 2
I1_PER_CHUNK = N_KEYS // N_I1_CHUNKS
COEF_PITCH = I1_PER_CHUNK + SUBLANES


def _cparams(sem):
    return pltpu.CompilerParams(dimension_semantics=sem, vmem_limit_bytes=VMEM_LIMIT_BYTES)


def _rms(x, g):
    return x * lax.rsqrt(jnp.mean(x * x, axis=-1, keepdims=True) + EPS) * g


def _gelu(x):
    return 0.5 * x * (1.0 + lax.erf(x * (1.0 / math.sqrt(2.0))))


def _sigmoid(x):
    return 1.0 / (1.0 + jnp.exp(-x))


def _in_proj_kernel(x_ref, g_ref, w_ref, qg_ref, kg_ref, cos_ref, sin_ref,
                    q_ref, k_ref, v_ref, xr_ref, gate_ref, hn_ref):
    j = pl.program_id(1)

    @pl.when(j == 0)
    def _():
        hn_ref[...] = _rms(x_ref[...], g_ref[...]).astype(BF16)

    y = jnp.dot(hn_ref[...], w_ref[...], preferred_element_type=F32)

    def head_norm_rope(g):
        cos = cos_ref[...]
        sin = sin_ref[...]
        outs = []
        for h in range(N_HEADS):
            t = _rms(y[:, h * HEAD_DIM:(h + 1) * HEAD_DIM], g)
            outs.append(t * cos + pltpu.roll(t, HEAD_DIM // 2, 1) * sin)
        return jnp.concatenate(outs, axis=-1)

    @pl.when(j == 0)
    def _():
        q_ref[...] = head_norm_rope(qg_ref[...])

    @pl.when(j == 1)
    def _():
        k_ref[...] = head_norm_rope(kg_ref[...])

    @pl.when(j == 2)
    def _():
        v_ref[...] = y

    @pl.when(j == 3)
    def _():
        xr_ref[...] = y

    @pl.when(j == 4)
    def _():
        gate_ref[...] = y


def _in_proj(x, ln1_g, w_in, qg, kg, cos, sin, seq):
    t = x.shape[0]
    tm = min(TM_IN, seq)
    nblk = seq // tm
    col = ATTN_WIDTH
    tok = lambda i, j: (i, 0)
    const = lambda i, j: (0, 0)
    pos = lambda i, j: (i % nblk, 0)
    out = jax.ShapeDtypeStruct((t, col), F32)
    return pl.pallas_call(
        _in_proj_kernel,
        grid=(t // tm, N_IN_BLOCKS),
        in_specs=[
            pl.BlockSpec((tm, D_MODEL), tok),
            pl.BlockSpec((1, D_MODEL), const),
            pl.BlockSpec((D_MODEL, col), lambda i, j: (0, j)),
            pl.BlockSpec((1, HEAD_DIM), const),
            pl.BlockSpec((1, HEAD_DIM), const),
            pl.BlockSpec((tm, HEAD_DIM), pos),
            pl.BlockSpec((tm, HEAD_DIM), pos),
        ],
        out_specs=[pl.BlockSpec((tm, col), tok)] * 5,
        out_shape=[out] * 5,
        scratch_shapes=[pltpu.VMEM((tm, D_MODEL), BF16)],
        compiler_params=_cparams(("arbitrary", "arbitrary")),
        name="in_proj",
    )(x, ln1_g, w_in, qg, kg, cos, sin)


def _attn_kernel(q_ref, kp_ref, kc_ref, kn_ref, vp_ref, vc_ref, vn_ref, o_ref,
                 acc_ref, m_ref, l_ref, *, n_tiles):
    i = pl.program_id(2)
    tp = q_ref.shape[0]
    scale = HEAD_DIM ** -0.5
    qq = lax.broadcasted_iota(I32, (TQ, 2 * TQ), 0)
    kk = lax.broadcasted_iota(I32, (TQ, 2 * TQ), 1)
    band = (kk >= qq) & (kk - qq <= 2 * HALF_WINDOW)
    first = i == 0
    last = i == n_tiles - 1

    for g, d in enumerate(DILATIONS):
        per_res = tp // d
        n_sub = per_res // TQ
        for r in range(d):
            for u in range(n_sub):
                def rows(ref, start, count):
                    if d == 1:
                        return ref[pl.ds(start, count), :]
                    return ref[pl.ds(r + d * start, count, stride=d), :]

                q = rows(q_ref, u * TQ, TQ)
                lo = u * TQ - HALF_WINDOW
                hi = lo + 2 * TQ
                kparts, vparts = [], []
                if lo < 0:
                    kparts.append(rows(kp_ref, per_res + lo, -lo))
                    vparts.append(rows(vp_ref, per_res + lo, -lo))
                c0, c1 = max(lo, 0), min(hi, per_res)
                kparts.append(rows(kc_ref, c0, c1 - c0))
                vparts.append(rows(vc_ref, c0, c1 - c0))
                if hi > per_res:
                    kparts.append(rows(kn_ref, 0, hi - per_res))
                    vparts.append(rows(vn_ref, 0, hi - per_res))
                kb = jnp.concatenate(kparts, axis=0).astype(BF16)
                vb = jnp.concatenate(vparts, axis=0).astype(BF16)
                s = lax.dot_general(q.astype(BF16), kb, (((1,), (1,)), ((), ())),
                                    preferred_element_type=F32) * scale
                ok = band
                if lo < 0:
                    ok = ok & ((kk >= -lo) | jnp.logical_not(first))
                if hi > per_res:
                    ok = ok & ((kk < 2 * TQ - (hi - per_res)) | jnp.logical_not(last))
                s = jnp.where(ok, s, NEG)
                m = jnp.max(s, axis=-1, keepdims=True)
                p = jnp.exp(s - m)
                l = jnp.sum(p, axis=-1, keepdims=True)
                o = jnp.dot(p.astype(BF16), vb, preferred_element_type=F32)
                if d == 1:
                    dst = pl.ds(u * TQ, TQ)
                else:
                    dst = pl.ds(r + d * u * TQ, TQ, stride=d)
                acc_ref[g, dst, :] = o
                m_ref[g, dst, :] = jnp.broadcast_to(m, (TQ, LANES))
                l_ref[g, dst, :] = jnp.broadcast_to(l, (TQ, LANES))

    m_all = jnp.maximum(jnp.maximum(m_ref[0], m_ref[1]), m_ref[2])
    num = jnp.zeros((tp, LANES), F32)
    den = jnp.zeros((tp, LANES), F32)
    for g in range(len(DILATIONS)):
        w = jnp.exp(m_ref[g] - m_all)
        num = num + w * acc_ref[g]
        den = den + w * l_ref[g]
    o_ref[...] = num / den


def _dil_attn(q, k, v):
    b, s, _ = q.shape
    tp = TP_ATTN
    n_tiles = s // tp
    cur = lambda bb, h, i: (bb, i, h)
    prv = lambda bb, h, i: (bb, jnp.maximum(i - 1, 0), h)
    nxt = lambda bb, h, i: (bb, jnp.minimum(i + 1, n_tiles - 1), h)
    blk = (None, tp, HEAD_DIM)
    return pl.pallas_call(
        functools.partial(_attn_kernel, n_tiles=n_tiles),
        grid=(b, N_HEADS, n_tiles),
        in_specs=[
            pl.BlockSpec(blk, cur),
            pl.BlockSpec(blk, prv), pl.BlockSpec(blk, cur), pl.BlockSpec(blk, nxt),
            pl.BlockSpec(blk, prv), pl.BlockSpec(blk, cur), pl.BlockSpec(blk, nxt),
        ],
        out_specs=pl.BlockSpec(blk, cur),
        out_shape=jax.ShapeDtypeStruct((b, s, ATTN_WIDTH), F32),
        scratch_shapes=[pltpu.VMEM((len(DILATIONS), tp, LANES), F32)] * 3,
        compiler_params=_cparams(("arbitrary", "arbitrary", "arbitrary")),
        name="dil_attn",
    )(q, k, k, k, v, v, v)


def _rglru_kernel(xf_p, xf_c, xf_n, xr_p, xr_c, xr_n, cw_ref, cb_ref, w_ref, ba_ref, bx_ref,
                  lam_ref, hf_ref, hr_ref, xe_ref, a_ref, b_ref, h_ref, carry_ref, *, n_tiles):
    i = pl.program_id(1)
    ts = xf_c.shape[0]

    @pl.when(i == 0)
    def _():
        carry_ref[...] = jnp.zeros_like(carry_ref)

    def gates(direction, xp, xc_ref, xn, at_start, at_end):
        xe_ref[0:SUBLANES, :] = jnp.where(at_start, 0.0, xp[...])
        xe_ref[SUBLANES:SUBLANES + ts, :] = xc_ref[...]
        xe_ref[SUBLANES + ts:2 * SUBLANES + ts, :] = jnp.where(at_end, 0.0, xn[...])
        base = SUBLANES - CONV_LEFT
        xc = cb_ref[...] + xe_ref[pl.ds(base, ts), :] * cw_ref[0:1, :]
        for jj in range(1, CONV_WIDTH):
            xc = xc + xe_ref[pl.ds(base + jj, ts), :] * cw_ref[jj:jj + 1, :]
        lam = lam_ref[direction:direction + 1, :]
        z = -lam
        softplus = jnp.maximum(z, 0.0) + jnp.log(1.0 + jnp.exp(-jnp.abs(z)))
        c = -LRU_C * softplus
        for g in range(N_LRU_BLOCKS):
            sl = slice(g * LRU_BLOCK, (g + 1) * LRU_BLOCK)
            xg = xc[:, sl]
            y = jnp.dot(xg.astype(BF16), w_ref[direction, g], preferred_element_type=F32)
            r = _sigmoid(y[:, :LRU_BLOCK] + ba_ref[direction:direction + 1, sl])
            gi = _sigmoid(y[:, LRU_BLOCK:] + bx_ref[direction:direction + 1, sl])
            log_a = c[:, sl] * r
            a = jnp.exp(log_a)
            bt = jnp.sqrt(-jnp.tanh(log_a) * (a * a + 1.0)) * gi * xg
            a_ref[direction, pl.ds(g, ts, stride=N_LRU_BLOCKS), :] = a
            b_ref[direction, pl.ds(g, ts, stride=N_LRU_BLOCKS), :] = bt

    gates(0, xf_p, xf_c, xf_n, i == 0, i == n_tiles - 1)
    gates(1, xr_p, xr_c, xr_n, i == n_tiles - 1, i == 0)

    unroll = 8

    def step(it, carry):
        hf, hr = carry
        for uu in range(unroll):
            tf = pl.multiple_of((it * unroll + uu) * SUBLANES, SUBLANES)
            tr = pl.multiple_of((ts - 1 - it * unroll - uu) * SUBLANES, SUBLANES)
            hf = a_ref[0, pl.ds(tf, SUBLANES), :] * hf + b_ref[0, pl.ds(tf, SUBLANES), :]
            hr = a_ref[1, pl.ds(tr, SUBLANES), :] * hr + b_ref[1, pl.ds(tr, SUBLANES), :]
            h_ref[0, pl.ds(tf, SUBLANES), :] = hf
            h_ref[1, pl.ds(tr, SUBLANES), :] = hr
        return hf, hr

    hf, hr = lax.fori_loop(0, ts // unroll, step, (carry_ref[0], carry_ref[1]))
    carry_ref[0] = hf
    carry_ref[1] = hr
    for g in range(N_LRU_BLOCKS):
        sl = slice(g * LRU_BLOCK, (g + 1) * LRU_BLOCK)
        hf_ref[:, sl] = h_ref[0, pl.ds(g, ts, stride=N_LRU_BLOCKS), :]
        hr_ref[:, sl] = h_ref[1, pl.ds(g, ts, stride=N_LRU_BLOCKS), :]


def _rglru(xr, conv_w, conv_b, w_cat, ba, bx, lam):
    b, s, _ = xr.shape
    ts = TS_LRU
    n_tiles = s // ts
    per = ts // SUBLANES
    n8 = s // SUBLANES
    halo = (None, SUBLANES, LRU_WIDTH)
    tile = (None, ts, LRU_WIDTH)
    rev = lambda i: n_tiles - 1 - i
    const2 = lambda bb, i: (0, 0)
    in_specs = [
        pl.BlockSpec(halo, lambda bb, i: (bb, jnp.maximum(i * per - 1, 0), 0)),
        pl.BlockSpec(tile, lambda bb, i: (bb, i, 0)),
        pl.BlockSpec(halo, lambda bb, i: (bb, jnp.minimum((i + 1) * per, n8 - 1), 0)),
        pl.BlockSpec(halo, lambda bb, i: (bb, jnp.maximum(rev(i) * per - 1, 0), 0)),
        pl.BlockSpec(tile, lambda bb, i: (bb, rev(i), 0)),
        pl.BlockSpec(halo, lambda bb, i: (bb, jnp.minimum((rev(i) + 1) * per, n8 - 1), 0)),
        pl.BlockSpec((CONV_WIDTH, LRU_WIDTH), const2),
        pl.BlockSpec((1, LRU_WIDTH), const2),
        pl.BlockSpec((2, N_LRU_BLOCKS, LRU_BLOCK, 2 * LRU_BLOCK), lambda bb, i: (0, 0, 0, 0)),
        pl.BlockSpec((2, LRU_WIDTH), const2),
        pl.BlockSpec((2, LRU_WIDTH), const2),
        pl.BlockSpec((2, LRU_WIDTH), const2),
    ]
    out = jax.ShapeDtypeStruct((b, s, LRU_WIDTH), F32)
    return pl.pallas_call(
        functools.partial(_rglru_kernel, n_tiles=n_tiles),
        grid=(b, n_tiles),
        in_specs=in_specs,
        out_specs=[pl.BlockSpec(tile, lambda bb, i: (bb, i, 0)),
                   pl.BlockSpec(tile, lambda bb, i: (bb, rev(i), 0))],
        out_shape=[out, out],
        scratch_shapes=[
            pltpu.VMEM((ts + 2 * SUBLANES, LRU_WIDTH), F32),
            pltpu.VMEM((2, ts * N_LRU_BLOCKS, LANES), F32),
            pltpu.VMEM((2, ts * N_LRU_BLOCKS, LANES), F32),
            pltpu.VMEM((2, ts * N_LRU_BLOCKS, LANES), F32),
            pltpu.VMEM((2, SUBLANES, LANES), F32),
        ],
        compiler_params=_cparams(("arbitrary", "arbitrary")),
        name="rglru",
    )(xr, xr, xr, xr, xr, xr, conv_w, conv_b, w_cat, ba, bx, lam)


def _out_proj_kernel(attn_ref, hf_ref, hr_ref, gate_ref, x_ref, ga_ref, gl_ref, w_ref, g2_ref,
                     x1_ref, xn_ref):
    lru = (hf_ref[...] + hr_ref[...]) * _gelu(gate_ref[...])
    na = _rms(attn_ref[...], ga_ref[...]).astype(BF16)
    nl = _rms(lru, gl_ref[...]).astype(BF16)
    mix = jnp.dot(jnp.concatenate([na, nl], axis=-1), w_ref[...], preferred_element_type=F32)
    x1 = x_ref[...] + mix
    x1_ref[...] = x1
    xn_ref[...] = _rms(x1, g2_ref[...]).astype(BF16)


def _out_proj(attn, hf, hr, gate, x, ga, gl, w_out, g2):
    t = x.shape[0]
    tm = TM_OUT
    tok = lambda i: (i, 0)
    const = lambda i: (0, 0)
    half = pl.BlockSpec((tm, ATTN_WIDTH), tok)
    full = pl.BlockSpec((tm, D_MODEL), tok)
    return pl.pallas_call(
        _out_proj_kernel,
        grid=(t // tm,),
        in_specs=[half, half, half, half, full,
                  pl.BlockSpec((1, ATTN_WIDTH), const), pl.BlockSpec((1, LRU_WIDTH), const),
                  pl.BlockSpec((D_MODEL, D_MODEL), const), pl.BlockSpec((1, D_MODEL), const)],
        out_specs=[full, full],
        out_shape=[jax.ShapeDtypeStruct((t, D_MODEL), F32), jax.ShapeDtypeStruct((t, D_MODEL), BF16)],
        compiler_params=_cparams(("arbitrary",)),
        name="out_proj",
    )(attn, hf, hr, gate, x, ga, gl, w_out, g2)


def _top16_rows(x, codes):
    big = jnp.iinfo(jnp.int32).max
    vals, ids = [], []
    for _ in range(PEER_TOPK):
        m = jnp.max(x, axis=0, keepdims=True)
        idx = jnp.min(jnp.where(x == m, codes, big), axis=0, keepdims=True)
        x = jnp.where(codes == idx, -jnp.inf, x)
        vals.append(m)
        ids.append(idx)
    return jnp.concatenate(vals, axis=0), jnp.concatenate(ids, axis=0)


def _peer_route_kernel(xn_ref, wq_ref, keys_ref, i1_ref, i2_ref, g_ref):
    tm = xn_ref.shape[0]
    pq = jnp.dot(xn_ref[...], wq_ref[...], preferred_element_type=F32).astype(BF16)
    key_iota = lax.broadcasted_iota(I32, (N_KEYS, tm), 0)
    row16 = lax.broadcasted_iota(I32, (PEER_TOPK, tm), 0)
    row8 = lax.broadcasted_iota(I32, (SUBLANES, tm), 0)
    cand_codes = jnp.concatenate(
        [row16] + [a * PEER_TOPK + row8 for a in range(1, 8)] + [(row8 + 8) * PEER_TOPK], axis=0)
    i1_all, i2_all, g_all = [], [], []
    for h in range(PEER_HEADS):
        sides = []
        for p in range(2):
            hp = 2 * h + p
            sc = lax.dot_general(keys_ref[hp], pq[:, hp * PEER_HALF:(hp + 1) * PEER_HALF],
                                 (((1,), (1,)), ((), ())), preferred_element_type=F32)
            sides.append(_top16_rows(sc, key_iota))
        (v1, id1), (v2, id2) = sides
        cand = jnp.concatenate(
            [v1[0:1] + v2] + [v1[a:a + 1] + v2[0:8] for a in range(1, 8)] + [v1[8:16] + v2[0:1]],
            axis=0)
        top, code = _top16_rows(cand, cand_codes)
        ra = code >> 4
        rb = code & (PEER_TOPK - 1)
        i1 = jnp.zeros((PEER_TOPK, tm), I32)
        i2 = jnp.zeros((PEER_TOPK, tm), I32)
        for a in range(PEER_TOPK):
            i1 = jnp.where(ra == a, id1[a:a + 1], i1)
            i2 = jnp.where(rb == a, id2[a:a + 1], i2)
        e = jnp.exp(top - top[0:1])
        g = e / jnp.sum(e, axis=0, keepdims=True)
        i1_all.append(i1)
        i2_all.append(i2)
        g_all.append(g)
    i1_ref[...] = jnp.concatenate(i1_all, axis=0).T
    i2_ref[...] = jnp.concatenate(i2_all, axis=0).T
    g_ref[...] = jnp.concatenate(g_all, axis=0).T


def _peer_route(xn, wq, keys):
    t = xn.shape[0]
    tm = TM_ROUTE
    tok = lambda i: (i, 0)
    term = pl.BlockSpec((tm, N_TERMS), tok)
    return pl.pallas_call(
        _peer_route_kernel,
        grid=(t // tm,),
        in_specs=[pl.BlockSpec((tm, D_MODEL), tok),
                  pl.BlockSpec((D_MODEL, D_MODEL), lambda i: (0, 0)),
                  pl.BlockSpec((2 * PEER_HEADS, N_KEYS, PEER_HALF), lambda i: (0, 0, 0))],
        out_specs=[term, term, term],
        out_shape=[jax.ShapeDtypeStruct((t, N_TERMS), I32), jax.ShapeDtypeStruct((t, N_TERMS), I32),
                   jax.ShapeDtypeStruct((t, N_TERMS), F32)],
        compiler_params=_cparams(("arbitrary",)),
        name="peer_route",
    )(xn, wq, keys)


def _peer_dense_kernel(xn_ref, x1_ref, i1_ref, i2_ref, g_ref, ut_ref, v_ref, y_ref, coef_ref):
    e = pl.program_id(1)
    tm = xn_ref.shape[0]
    blocks_per_chunk = (N_EXPERTS // EB_PEER) // N_I1_CHUNKS
    i1_per_block = EB_PEER // N_KEYS

    @pl.when(e == 0)
    def _():
        y_ref[...] = x1_ref[...]

    @pl.when(e % blocks_per_chunk == 0)
    def _():
        i1_base = (e // blocks_per_chunk) * I1_PER_CHUNK
        rows1 = lax.broadcasted_iota(I32, (I1_PER_CHUNK, N_TERMS), 0) + i1_base
        rows2 = lax.broadcasted_iota(I32, (N_KEYS, N_TERMS), 0)

        def one_token(t, carry):
            i1 = i1_ref[pl.ds(t, 1), :]
            i2 = i2_ref[pl.ds(t, 1), :]
            g = g_ref[pl.ds(t, 1), :]
            a1 = jnp.where(rows1 == i1, g, 0.0).astype(BF16)
            a2 = jnp.where(rows2 == i2, 1.0, 0.0).astype(BF16)
            c = lax.dot_general(a1, a2, (((1,), (1,)), ((), ())), preferred_element_type=F32)
            coef_ref[pl.ds(pl.multiple_of(t * COEF_PITCH, SUBLANES), I1_PER_CHUNK), :] = c
            return carry

        lax.fori_loop(0, tm, one_token, 0)

    hid = jnp.dot(xn_ref[...], ut_ref[...], preferred_element_type=F32)
    act = _gelu(hid)
    local = (e % blocks_per_chunk) * i1_per_block
    parts = []
    for jj in range(i1_per_block):
        cf = coef_ref[pl.ds(local + jj, tm, stride=COEF_PITCH), :]
        parts.append((cf * act[:, jj * N_KEYS:(jj + 1) * N_KEYS]).astype(BF16))
    y_ref[...] += jnp.dot(jnp.concatenate(parts, axis=-1), v_ref[...], preferred_element_type=F32)


def _peer_dense(xn, x1, i1, i2, g, ut, v):
    t = xn.shape[0]
    tm = TM_PEER
    tok = lambda i, e: (i, 0)
    term = pl.BlockSpec((tm, N_TERMS), tok)
    return pl.pallas_call(
        _peer_dense_kernel,
        grid=(t // tm, N_EXPERTS // EB_PEER),
        in_specs=[pl.BlockSpec((tm, D_MODEL), tok),
                  pl.BlockSpec((tm, D_MODEL), tok),
                  term, term, term,
                  pl.BlockSpec((D_MODEL, EB_PEER), lambda i, e: (0, e)),
                  pl.BlockSpec((EB_PEER, D_MODEL), lambda i, e: (e, 0))],
        out_specs=pl.BlockSpec((tm, D_MODEL), tok),
        out_shape=jax.ShapeDtypeStruct((t, D_MODEL), F32),
        scratch_shapes=[pltpu.VMEM((tm * COEF_PITCH, LANES), F32)],
        compiler_params=_cparams(("arbitrary", "arbitrary")),
        name="peer_dense",
    )(xn, x1, i1, i2, g, ut, v)


def _rope_tables(seq):
    inv = ROPE_THETA ** (-jnp.arange(0, HEAD_DIM, 2, dtype=F32) / HEAD_DIM)
    ang = jnp.arange(seq, dtype=F32)[:, None] * inv[None, :]
    cos, sin = jnp.cos(ang), jnp.sin(ang)
    return jnp.concatenate([cos, cos], axis=-1), jnp.concatenate([-sin, sin], axis=-1)


def _prep_layer(p):
    (ln1_g, w_in, qg, kg, conv_w, conv_b, wa, ba, wx, bx, lam, ga, gl, w_out, g2, wq, keys, u, v) = p
    w_cat = jnp.concatenate([wa, wx], axis=-1).astype(BF16)
    return dict(
        ln1_g=ln1_g.reshape(1, -1), w_in=w_in.astype(BF16), qg=qg.reshape(1, -1), kg=kg.reshape(1, -1),
        conv_w=conv_w, conv_b=conv_b.reshape(1, -1), w_cat=w_cat, ba=ba, bx=bx, lam=lam,
        ga=ga.reshape(1, -1), gl=gl.reshape(1, -1), w_out=w_out.astype(BF16), g2=g2.reshape(1, -1),
        wq=wq.astype(BF16), keys=keys.reshape(2 * PEER_HEADS, N_KEYS, PEER_HALF).astype(BF16),
        ut=u.astype(BF16).T, v=v.astype(BF16))


def _layer(x, w, cos, sin):
    b, s, _ = x.shape
    xt = x.reshape(b * s, D_MODEL)
    q, k, v, xr, gate = _in_proj(xt, w["ln1_g"], w["w_in"], w["qg"], w["kg"], cos, sin, s)
    shp = (b, s, ATTN_WIDTH)
    attn = _dil_attn(q.reshape(shp), k.reshape(shp), v.reshape(shp))
    hf, hr = _rglru(xr.reshape(shp), w["conv_w"], w["conv_b"], w["w_cat"], w["ba"], w["bx"], w["lam"])
    x1, xn = _out_proj(attn.reshape(b * s, -1), hf.reshape(b * s, -1), hr.reshape(b * s, -1), gate, xt,
                       w["ga"], w["gl"], w["w_out"], w["g2"])
    i1, i2, g = _peer_route(xn, w["wq"], w["keys"])
    y = _peer_dense(xn, x1, i1, i2, g, w["ut"], w["v"])
    return y.reshape(b, s, D_MODEL)


def kernel(x_prompt, x_sample, ln1_g, w_in, q_norm_g, k_norm_g, conv_w, conv_b, lru_wa, lru_ba, lru_wx, lru_bx, lru_lam, attn_out_g, lru_out_g, w_out, ln2_g, peer_wq, peer_keys, peer_u, peer_v):
    params = (ln1_g, w_in, q_norm_g, k_norm_g, conv_w, conv_b, lru_wa, lru_ba, lru_wx, lru_bx,
              lru_lam, attn_out_g, lru_out_g, w_out, ln2_g, peer_wq, peer_keys, peer_u, peer_v)
    depth = ln1_g.shape[0]
    layers = [_prep_layer([p[l] for p in params]) for l in range(depth)]

    def trunk(x):
        cos, sin = _rope_tables(x.shape[1])
        for w in layers:
            x = _layer(x, w, cos, sin)
        return x

    return (trunk(x_prompt), trunk(x_sample))
```

```python
import functools
import math

import jax
import jax.numpy as jnp
from jax import lax
from jax.experimental import pallas as pl
from jax.experimental.pallas import tpu as pltpu

F32 = jnp.float32
BF16 = jnp.bfloat16
I32 = jnp.int32

D_MODEL = 2048
HEAD_DIM = 128
N_HEADS = 8
ATTN_WIDTH = N_HEADS * HEAD_DIM
LRU_WIDTH = D_MODEL - ATTN_WIDTH
N_LRU_BLOCKS = 8
LRU_BLOCK = LRU_WIDTH // N_LRU_BLOCKS
CONV_WIDTH = 4
CONV_LEFT = 2
LRU_C = 8.0
DILATIONS = (1, 4, 16)
HALF_WINDOW = 64
ROPE_THETA = 10000.0
N_KEYS = 128
N_EXPERTS = N_KEYS * N_KEYS
PEER_HEADS = 8
PEER_TOPK = 16
PEER_HALF = 128
N_TERMS = PEER_HEADS * PEER_TOPK
EPS = 1e-6
NEG = -1e30

LANES = 128
SUBLANES = 8
VMEM_LIMIT_BYTES = 56 * 1024 * 1024

TM_IN = 512
N_IN_BLOCKS = 5
TP_ATTN = 2048
TQ = 128
TS_LRU = 512
TM_OUT = 256
TM_ROUTE = 256
TM_PEER = 512
EB_PEER = 512
N_I1_CHUNKS = 2
I1_PER_CHUNK = N_KEYS // N_I1_CHUNKS
COEF_PITCH = I1_PER_CHUNK + SUBLANES
COEF_GROUP = 32


def _cparams(sem):
    return pltpu.CompilerParams(dimension_semantics=sem, vmem_limit_bytes=VMEM_LIMIT_BYTES)


def _rms(x, g):
    return x * lax.rsqrt(jnp.mean(x * x, axis=-1, keepdims=True) + EPS) * g


def _gelu(x):
    return 0.5 * x * (1.0 + lax.erf(x * (1.0 / math.sqrt(2.0))))


def _sigmoid(x):
    return 1.0 / (1.0 + jnp.exp(-x))


def _in_proj_kernel(x_ref, g_ref, w_ref, qg_ref, kg_ref, cos_ref, sin_ref,
                    q_ref, k_ref, v_ref, xr_ref, gate_ref, hn_ref):
    j = pl.program_id(1)

    @pl.when(j == 0)
    def _():
        hn_ref[...] = _rms(x_ref[...], g_ref[...]).astype(BF16)

    y = jnp.dot(hn_ref[...], w_ref[...], preferred_element_type=F32)

    def head_norm_rope(g):
        cos = cos_ref[...]
        sin = sin_ref[...]
        outs = []
        for h in range(N_HEADS):
            t = _rms(y[:, h * HEAD_DIM:(h + 1) * HEAD_DIM], g)
            outs.append(t * cos + pltpu.roll(t, HEAD_DIM // 2, 1) * sin)
        return jnp.concatenate(outs, axis=-1)

    @pl.when(j == 0)
    def _():
        q_ref[...] = head_norm_rope(qg_ref[...])

    @pl.when(j == 1)
    def _():
        k_ref[...] = head_norm_rope(kg_ref[...])

    @pl.when(j == 2)
    def _():
        v_ref[...] = y

    @pl.when(j == 3)
    def _():
        xr_ref[...] = y

    @pl.when(j == 4)
    def _():
        gate_ref[...] = y


def _in_proj(x, ln1_g, w_in, qg, kg, cos, sin, seq):
    t = x.shape[0]
    tm = min(TM_IN, seq)
    nblk = seq // tm
    col = ATTN_WIDTH
    tok = lambda i, j: (i, 0)
    const = lambda i, j: (0, 0)
    pos = lambda i, j: (i % nblk, 0)
    out = jax.ShapeDtypeStruct((t, col), F32)
    return pl.pallas_call(
        _in_proj_kernel,
        grid=(t // tm, N_IN_BLOCKS),
        in_specs=[
            pl.BlockSpec((tm, D_MODEL), tok),
            pl.BlockSpec((1, D_MODEL), const),
            pl.BlockSpec((D_MODEL, col), lambda i, j: (0, j)),
            pl.BlockSpec((1, HEAD_DIM), const),
            pl.BlockSpec((1, HEAD_DIM), const),
            pl.BlockSpec((tm, HEAD_DIM), pos),
            pl.BlockSpec((tm, HEAD_DIM), pos),
        ],
        out_specs=[pl.BlockSpec((tm, col), tok)] * 5,
        out_shape=[out] * 5,
        scratch_shapes=[pltpu.VMEM((tm, D_MODEL), BF16)],
        compiler_params=_cparams(("arbitrary", "arbitrary")),
        name="in_proj",
    )(x, ln1_g, w_in, qg, kg, cos, sin)


def _attn_kernel(q_ref, kp_ref, kc_ref, kn_ref, vp_ref, vc_ref, vn_ref, o_ref,
                 acc_ref, m_ref, l_ref, *, n_tiles):
    i = pl.program_id(2)
    tp = q_ref.shape[0]
    scale = HEAD_DIM ** -0.5
    qq = lax.broadcasted_iota(I32, (TQ, 2 * TQ), 0)
    kk = lax.broadcasted_iota(I32, (TQ, 2 * TQ), 1)
    band = (kk >= qq) & (kk - qq <= 2 * HALF_WINDOW)
    first = i == 0
    last = i == n_tiles - 1

    for g, d in enumerate(DILATIONS):
        per_res = tp // d
        n_sub = per_res // TQ
        for r in range(d):
            for u in range(n_sub):
                def rows(ref, start, count):
                    if d == 1:
                        return ref[pl.ds(start, count), :]
                    return ref[pl.ds(r + d * start, count, stride=d), :]

                q = rows(q_ref, u * TQ, TQ)
                lo = u * TQ - HALF_WINDOW
                hi = lo + 2 * TQ
                kparts, vparts = [], []
                if lo < 0:
                    kparts.append(rows(kp_ref, per_res + lo, -lo))
                    vparts.append(rows(vp_ref, per_res + lo, -lo))
                c0, c1 = max(lo, 0), min(hi, per_res)
                kparts.append(rows(kc_ref, c0, c1 - c0))
                vparts.append(rows(vc_ref, c0, c1 - c0))
                if hi > per_res:
                    kparts.append(rows(kn_ref, 0, hi - per_res))
                    vparts.append(rows(vn_ref, 0, hi - per_res))
                kb = jnp.concatenate(kparts, axis=0).astype(BF16)
                vb = jnp.concatenate(vparts, axis=0).astype(BF16)
                s = lax.dot_general(q.astype(BF16), kb, (((1,), (1,)), ((), ())),
                                    preferred_element_type=F32) * scale
                ok = band
                if lo < 0:
                    ok = ok & ((kk >= -lo) | jnp.logical_not(first))
                if hi > per_res:
                    ok = ok & ((kk < 2 * TQ - (hi - per_res)) | jnp.logical_not(last))
                s = jnp.where(ok, s, NEG)
                m = jnp.max(s, axis=-1, keepdims=True)
                p = jnp.exp(s - m)
                l = jnp.sum(p, axis=-1, keepdims=True)
                o = jnp.dot(p.astype(BF16), vb, preferred_element_type=F32)
                if d == 1:
                    dst = pl.ds(u * TQ, TQ)
                else:
                    dst = pl.ds(r + d * u * TQ, TQ, stride=d)
                acc_ref[g, dst, :] = o
                m_ref[g, dst, :] = jnp.broadcast_to(m, (TQ, LANES))
                l_ref[g, dst, :] = jnp.broadcast_to(l, (TQ, LANES))

    m_all = jnp.maximum(jnp.maximum(m_ref[0], m_ref[1]), m_ref[2])
    num = jnp.zeros((tp, LANES), F32)
    den = jnp.zeros((tp, LANES), F32)
    for g in range(len(DILATIONS)):
        w = jnp.exp(m_ref[g] - m_all)
        num = num + w * acc_ref[g]
        den = den + w * l_ref[g]
    o_ref[...] = num / den


def _dil_attn(q, k, v):
    b, s, _ = q.shape
    tp = TP_ATTN
    n_tiles = s // tp
    cur = lambda bb, h, i: (bb, i, h)
    prv = lambda bb, h, i: (bb, jnp.maximum(i - 1, 0), h)
    nxt = lambda bb, h, i: (bb, jnp.minimum(i + 1, n_tiles - 1), h)
    blk = (None, tp, HEAD_DIM)
    return pl.pallas_call(
        functools.partial(_attn_kernel, n_tiles=n_tiles),
        grid=(b, N_HEADS, n_tiles),
        in_specs=[
            pl.BlockSpec(blk, cur),
            pl.BlockSpec(blk, prv), pl.BlockSpec(blk, cur), pl.BlockSpec(blk, nxt),
            pl.BlockSpec(blk, prv), pl.BlockSpec(blk, cur), pl.BlockSpec(blk, nxt),
        ],
        out_specs=pl.BlockSpec(blk, cur),
        out_shape=jax.ShapeDtypeStruct((b, s, ATTN_WIDTH), F32),
        scratch_shapes=[pltpu.VMEM((len(DILATIONS), tp, LANES), F32)] * 3,
        compiler_params=_cparams(("arbitrary", "arbitrary", "arbitrary")),
        name="dil_attn",
    )(q, k, k, k, v, v, v)


def _rglru_kernel(xf_p, xf_c, xf_n, xr_p, xr_c, xr_n, cw_ref, cb_ref, w_ref, ba_ref, bx_ref,
                  lam_ref, hf_ref, hr_ref, xe_ref, a_ref, b_ref, h_ref, carry_ref, *, n_tiles):
    i = pl.program_id(1)
    ts = xf_c.shape[0]

    @pl.when(i == 0)
    def _():
        carry_ref[...] = jnp.zeros_like(carry_ref)

    def gates(direction, xp, xc_ref, xn, at_start, at_end):
        xe_ref[0:SUBLANES, :] = jnp.where(at_start, 0.0, xp[...])
        xe_ref[SUBLANES:SUBLANES + ts, :] = xc_ref[...]
        xe_ref[SUBLANES + ts:2 * SUBLANES + ts, :] = jnp.where(at_end, 0.0, xn[...])
        base = SUBLANES - CONV_LEFT
        xc = cb_ref[...] + xe_ref[pl.ds(base, ts), :] * cw_ref[0:1, :]
        for jj in range(1, CONV_WIDTH):
            xc = xc + xe_ref[pl.ds(base + jj, ts), :] * cw_ref[jj:jj + 1, :]
        lam = lam_ref[direction:direction + 1, :]
        z = -lam
        softplus = jnp.maximum(z, 0.0) + jnp.log(1.0 + jnp.exp(-jnp.abs(z)))
        c = -LRU_C * softplus
        for g in range(N_LRU_BLOCKS):
            sl = slice(g * LRU_BLOCK, (g + 1) * LRU_BLOCK)
            xg = xc[:, sl]
            y = jnp.dot(xg.astype(BF16), w_ref[direction, g], preferred_element_type=F32)
            r = _sigmoid(y[:, :LRU_BLOCK] + ba_ref[direction:direction + 1, sl])
            gi = _sigmoid(y[:, LRU_BLOCK:] + bx_ref[direction:direction + 1, sl])
            log_a = c[:, sl] * r
            a = jnp.exp(log_a)
            bt = jnp.sqrt(-jnp.tanh(log_a) * (a * a + 1.0)) * gi * xg
            a_ref[direction, pl.ds(g, ts, stride=N_LRU_BLOCKS), :] = a
            b_ref[direction, pl.ds(g, ts, stride=N_LRU_BLOCKS), :] = bt

    gates(0, xf_p, xf_c, xf_n, i == 0, i == n_tiles - 1)
    gates(1, xr_p, xr_c, xr_n, i == n_tiles - 1, i == 0)

    unroll = 8

    def step(it, carry):
        hf, hr = carry
        for uu in range(unroll):
            tf = pl.multiple_of((it * unroll + uu) * SUBLANES, SUBLANES)
            tr = pl.multiple_of((ts - 1 - it * unroll - uu) * SUBLANES, SUBLANES)
            hf = a_ref[0, pl.ds(tf, SUBLANES), :] * hf + b_ref[0, pl.ds(tf, SUBLANES), :]
            hr = a_ref[1, pl.ds(tr, SUBLANES), :] * hr + b_ref[1, pl.ds(tr, SUBLANES), :]
            h_ref[0, pl.ds(tf, SUBLANES), :] = hf
            h_ref[1, pl.ds(tr, SUBLANES), :] = hr
        return hf, hr

    hf, hr = lax.fori_loop(0, ts // unroll, step, (carry_ref[0], carry_ref[1]))
    carry_ref[0] = hf
    carry_ref[1] = hr
    for g in range(N_LRU_BLOCKS):
        sl = slice(g * LRU_BLOCK, (g + 1) * LRU_BLOCK)
        hf_ref[:, sl] = h_ref[0, pl.ds(g, ts, stride=N_LRU_BLOCKS), :]
        hr_ref[:, sl] = h_ref[1, pl.ds(g, ts, stride=N_LRU_BLOCKS), :]


def _rglru(xr, conv_w, conv_b, w_cat, ba, bx, lam):
    b, s, _ = xr.shape
    ts = TS_LRU
    n_tiles = s // ts
    per = ts // SUBLANES
    n8 = s // SUBLANES
    halo = (None, SUBLANES, LRU_WIDTH)
    tile = (None, ts, LRU_WIDTH)
    rev = lambda i: n_tiles - 1 - i
    const2 = lambda bb, i: (0, 0)
    in_specs = [
        pl.BlockSpec(halo, lambda bb, i: (bb, jnp.maximum(i * per - 1, 0), 0)),
        pl.BlockSpec(tile, lambda bb, i: (bb, i, 0)),
        pl.BlockSpec(halo, lambda bb, i: (bb, jnp.minimum((i + 1) * per, n8 - 1), 0)),
        pl.BlockSpec(halo, lambda bb, i: (bb, jnp.maximum(rev(i) * per - 1, 0), 0)),
        pl.BlockSpec(tile, lambda bb, i: (bb, rev(i), 0)),
        pl.BlockSpec(halo, lambda bb, i: (bb, jnp.minimum((rev(i) + 1) * per, n8 - 1), 0)),
        pl.BlockSpec((CONV_WIDTH, LRU_WIDTH), const2),
        pl.BlockSpec((1, LRU_WIDTH), const2),
        pl.BlockSpec((2, N_LRU_BLOCKS, LRU_BLOCK, 2 * LRU_BLOCK), lambda bb, i: (0, 0, 0, 0)),
        pl.BlockSpec((2, LRU_WIDTH), const2),
        pl.BlockSpec((2, LRU_WIDTH), const2),
        pl.BlockSpec((2, LRU_WIDTH), const2),
    ]
    out = jax.ShapeDtypeStruct((b, s, LRU_WIDTH), F32)
    return pl.pallas_call(
        functools.partial(_rglru_kernel, n_tiles=n_tiles),
        grid=(b, n_tiles),
        in_specs=in_specs,
        out_specs=[pl.BlockSpec(tile, lambda bb, i: (bb, i, 0)),
                   pl.BlockSpec(tile, lambda bb, i: (bb, rev(i), 0))],
        out_shape=[out, out],
        scratch_shapes=[
            pltpu.VMEM((ts + 2 * SUBLANES, LRU_WIDTH), F32),
            pltpu.VMEM((2, ts * N_LRU_BLOCKS, LANES), F32),
            pltpu.VMEM((2, ts * N_LRU_BLOCKS, LANES), F32),
            pltpu.VMEM((2, ts * N_LRU_BLOCKS, LANES), F32),
            pltpu.VMEM((2, SUBLANES, LANES), F32),
        ],
        compiler_params=_cparams(("arbitrary", "arbitrary")),
        name="rglru",
    )(xr, xr, xr, xr, xr, xr, conv_w, conv_b, w_cat, ba, bx, lam)


def _out_proj_kernel(attn_ref, hf_ref, hr_ref, gate_ref, x_ref, ga_ref, gl_ref, w_ref, g2_ref,
                     x1_ref, xn_ref):
    lru = (hf_ref[...] + hr_ref[...]) * _gelu(gate_ref[...])
    na = _rms(attn_ref[...], ga_ref[...]).astype(BF16)
    nl = _rms(lru, gl_ref[...]).astype(BF16)
    mix = jnp.dot(jnp.concatenate([na, nl], axis=-1), w_ref[...], preferred_element_type=F32)
    x1 = x_ref[...] + mix
    x1_ref[...] = x1
    xn_ref[...] = _rms(x1, g2_ref[...]).astype(BF16)


def _out_proj(attn, hf, hr, gate, x, ga, gl, w_out, g2):
    t = x.shape[0]
    tm = TM_OUT
    tok = lambda i: (i, 0)
    const = lambda i: (0, 0)
    half = pl.BlockSpec((tm, ATTN_WIDTH), tok)
    full = pl.BlockSpec((tm, D_MODEL), tok)
    return pl.pallas_call(
        _out_proj_kernel,
        grid=(t // tm,),
        in_specs=[half, half, half, half, full,
                  pl.BlockSpec((1, ATTN_WIDTH), const), pl.BlockSpec((1, LRU_WIDTH), const),
                  pl.BlockSpec((D_MODEL, D_MODEL), const), pl.BlockSpec((1, D_MODEL), const)],
        out_specs=[full, full],
        out_shape=[jax.ShapeDtypeStruct((t, D_MODEL), F32), jax.ShapeDtypeStruct((t, D_MODEL), BF16)],
        compiler_params=_cparams(("arbitrary",)),
        name="out_proj",
    )(attn, hf, hr, gate, x, ga, gl, w_out, g2)


def _top16_rows(x, codes):
    big = jnp.iinfo(jnp.int32).max
    vals, ids = [], []
    for _ in range(PEER_TOPK):
        m = jnp.max(x, axis=0, keepdims=True)
        idx = jnp.min(jnp.where(x == m, codes, big), axis=0, keepdims=True)
        x = jnp.where(codes == idx, -jnp.inf, x)
        vals.append(m)
        ids.append(idx)
    return jnp.concatenate(vals, axis=0), jnp.concatenate(ids, axis=0)


def _peer_route_kernel(xn_ref, wq_ref, keys_ref, i1_ref, i2_ref, g_ref):
    tm = xn_ref.shape[0]
    pq = jnp.dot(xn_ref[...], wq_ref[...], preferred_element_type=F32).astype(BF16)
    key_iota = lax.broadcasted_iota(I32, (N_KEYS, tm), 0)
    row16 = lax.broadcasted_iota(I32, (PEER_TOPK, tm), 0)
    row8 = lax.broadcasted_iota(I32, (SUBLANES, tm), 0)
    cand_codes = jnp.concatenate(
        [row16] + [a * PEER_TOPK + row8 for a in range(1, 8)] + [(row8 + 8) * PEER_TOPK], axis=0)
    i1_all, i2_all, g_all = [], [], []
    for h in range(PEER_HEADS):
        sides = []
        for p in range(2):
            hp = 2 * h + p
            sc = lax.dot_general(keys_ref[hp], pq[:, hp * PEER_HALF:(hp + 1) * PEER_HALF],
                                 (((1,), (1,)), ((), ())), preferred_element_type=F32)
            sides.append(_top16_rows(sc, key_iota))
        (v1, id1), (v2, id2) = sides
        cand = jnp.concatenate(
            [v1[0:1] + v2] + [v1[a:a + 1] + v2[0:8] for a in range(1, 8)] + [v1[8:16] + v2[0:1]],
            axis=0)
        top, code = _top16_rows(cand, cand_codes)
        ra = code >> 4
        rb = code & (PEER_TOPK - 1)
        i1 = jnp.zeros((PEER_TOPK, tm), I32)
        i2 = jnp.zeros((PEER_TOPK, tm), I32)
        for a in range(PEER_TOPK):
            i1 = jnp.where(ra == a, id1[a:a + 1], i1)
            i2 = jnp.where(rb == a, id2[a:a + 1], i2)
        e = jnp.exp(top - top[0:1])
        g = e / jnp.sum(e, axis=0, keepdims=True)
        i1_all.append(i1)
        i2_all.append(i2)
        g_all.append(g)
    i1_ref[...] = jnp.concatenate(i1_all, axis=0).T
    i2_ref[...] = jnp.concatenate(i2_all, axis=0).T
    g_ref[...] = jnp.concatenate(g_all, axis=0).T


def _peer_route(xn, wq, keys):
    t = xn.shape[0]
    tm = TM_ROUTE
    tok = lambda i: (i, 0)
    term = pl.BlockSpec((tm, N_TERMS), tok)
    return pl.pallas_call(
        _peer_route_kernel,
        grid=(t // tm,),
        in_specs=[pl.BlockSpec((tm, D_MODEL), tok),
                  pl.BlockSpec((D_MODEL, D_MODEL), lambda i: (0, 0)),
                  pl.BlockSpec((2 * PEER_HEADS, N_KEYS, PEER_HALF), lambda i: (0, 0, 0))],
        out_specs=[term, term, term],
        out_shape=[jax.ShapeDtypeStruct((t, N_TERMS), I32), jax.ShapeDtypeStruct((t, N_TERMS), I32),
                   jax.ShapeDtypeStruct((t, N_TERMS), F32)],
        compiler_params=_cparams(("arbitrary",)),
        name="peer_route",
    )(xn, wq, keys)


def _peer_dense_kernel(xn_ref, x1_ref, i1_ref, i2_ref, g_ref, ut_ref, v_ref, y_ref, coef_ref):
    e = pl.program_id(1)
    tm = xn_ref.shape[0]
    blocks_per_chunk = (N_EXPERTS // EB_PEER) // N_I1_CHUNKS
    i1_per_block = EB_PEER // N_KEYS

    @pl.when(e == 0)
    def _():
        y_ref[...] = x1_ref[...]

    @pl.when(e % blocks_per_chunk == 0)
    def _():
        i1_base = (e // blocks_per_chunk) * I1_PER_CHUNK
        rows1 = lax.broadcasted_iota(I32, (I1_PER_CHUNK, N_TERMS), 0) + i1_base
        rows2 = lax.broadcasted_iota(I32, (N_KEYS, N_TERMS), 0)

        def token_group(gi, carry):
            t0 = pl.multiple_of(gi * COEF_GROUP, COEF_GROUP)
            i1b = i1_ref[pl.ds(t0, COEF_GROUP), :]
            i2b = i2_ref[pl.ds(t0, COEF_GROUP), :]
            gb = g_ref[pl.ds(t0, COEF_GROUP), :]
            for uu in range(COEF_GROUP):
                a1 = jnp.where(rows1 == i1b[uu:uu + 1, :], gb[uu:uu + 1, :], 0.0).astype(BF16)
                a2 = jnp.where(rows2 == i2b[uu:uu + 1, :], 1.0, 0.0).astype(BF16)
                c = lax.dot_general(a1, a2, (((1,), (1,)), ((), ())), preferred_element_type=F32)
                row = pl.multiple_of((t0 + uu) * COEF_PITCH, SUBLANES)
                coef_ref[pl.ds(row, I1_PER_CHUNK), :] = c
            return carry

        lax.fori_loop(0, tm // COEF_GROUP, token_group, 0)

    hid = jnp.dot(xn_ref[...], ut_ref[...], preferred_element_type=F32)
    act = _gelu(hid)
    local = (e % blocks_per_chunk) * i1_per_block
    parts = []
    for jj in range(i1_per_block):
        cf = coef_ref[pl.ds(local + jj, tm, stride=COEF_PITCH), :]
        parts.append((cf * act[:, jj * N_KEYS:(jj + 1) * N_KEYS]).astype(BF16))
    y_ref[...] += jnp.dot(jnp.concatenate(parts, axis=-1), v_ref[...], preferred_element_type=F32)


def _peer_dense(xn, x1, i1, i2, g, ut, v):
    t = xn.shape[0]
    tm = TM_PEER
    tok = lambda i, e: (i, 0)
    term = pl.BlockSpec((tm, N_TERMS), tok)
    return pl.pallas_call(
        _peer_dense_kernel,
        grid=(t // tm, N_EXPERTS // EB_PEER),
        in_specs=[pl.BlockSpec((tm, D_MODEL), tok),
                  pl.BlockSpec((tm, D_MODEL), tok),
                  term, term, term,
                  pl.BlockSpec((D_MODEL, EB_PEER), lambda i, e: (0, e)),
                  pl.BlockSpec((EB_PEER, D_MODEL), lambda i, e: (e, 0))],
        out_specs=pl.BlockSpec((tm, D_MODEL), tok),
        out_shape=jax.ShapeDtypeStruct((t, D_MODEL), F32),
        scratch_shapes=[pltpu.VMEM((tm * COEF_PITCH, LANES), F32)],
        compiler_params=_cparams(("arbitrary", "arbitrary")),
        name="peer_dense",
    )(xn, x1, i1, i2, g, ut, v)


def _rope_tables(seq):
    inv = ROPE_THETA ** (-jnp.arange(0, HEAD_DIM, 2, dtype=F32) / HEAD_DIM)
    ang = jnp.arange(seq, dtype=F32)[:, None] * inv[None, :]
    cos, sin = jnp.cos(ang), jnp.sin(ang)
    return jnp.concatenate([cos, cos], axis=-1), jnp.concatenate([-sin, sin], axis=-1)


def _prep_layer(p):
    (ln1_g, w_in, qg, kg, conv_w, conv_b, wa, ba, wx, bx, lam, ga, gl, w_out, g2, wq, keys, u, v) = p
    w_cat = jnp.concatenate([wa, wx], axis=-1).astype(BF16)
    return dict(
        ln1_g=ln1_g.reshape(1, -1), w_in=w_in.astype(BF16), qg=qg.reshape(1, -1), kg=kg.reshape(1, -1),
        conv_w=conv_w, conv_b=conv_b.reshape(1, -1), w_cat=w_cat, ba=ba, bx=bx, lam=lam,
        ga=ga.reshape(1, -1), gl=gl.reshape(1, -1), w_out=w_out.astype(BF16), g2=g2.reshape(1, -1),
        wq=wq.astype(BF16), keys=keys.reshape(2 * PEER_HEADS, N_KEYS, PEER_HALF).astype(BF16),
        ut=u.astype(BF16).T, v=v.astype(BF16))


def _layer(x, w, cos, sin):
    b, s, _ = x.shape
    xt = x.reshape(b * s, D_MODEL)
    q, k, v, xr, gate = _in_proj(xt, w["ln1_g"], w["w_in"], w["qg"], w["kg"], cos, sin, s)
    shp = (b, s, ATTN_WIDTH)
    attn = _dil_attn(q.reshape(shp), k.reshape(shp), v.reshape(shp))
    hf, hr = _rglru(xr.reshape(shp), w["conv_w"], w["conv_b"], w["w_cat"], w["ba"], w["bx"], w["lam"])
    x1, xn = _out_proj(attn.reshape(b * s, -1), hf.reshape(b * s, -1), hr.reshape(b * s, -1), gate, xt,
                       w["ga"], w["gl"], w["w_out"], w["g2"])
    i1, i2, g = _peer_route(xn, w["wq"], w["keys"])
    y = _peer_dense(xn, x1, i1, i2, g, w["ut"], w["v"])
    return y.reshape(b, s, D_MODEL)


def kernel(x_prompt, x_sample, ln1_g, w_in, q_norm_g, k_norm_g, conv_w, conv_b, lru_wa, lru_ba, lru_wx, lru_bx, lru_lam, attn_out_g, lru_out_g, w_out, ln2_g, peer_wq, peer_keys, peer_u, peer_v):
    params = (ln1_g, w_in, q_norm_g, k_norm_g, conv_w, conv_b, lru_wa, lru_ba, lru_wx, lru_bx,
              lru_lam, attn_out_g, lru_out_g, w_out, ln2_g, peer_wq, peer_keys, peer_u, peer_v)
    depth = ln1_g.shape[0]
    layers = [_prep_layer([p[l] for p in params]) for l in range(depth)]

    def trunk(x):
        cos, sin = _rope_tables(x.shape[1])
        for w in layers:
            x = _layer(x, w, cos, sin)
        return x

    return (trunk(x_prompt), trunk(x_sample))
```

```python
import functools
import math

import jax
import jax.numpy as jnp
from jax import lax
from jax.experimental import pallas as pl
from jax.experimental.pallas import tpu as pltpu

F32 = jnp.float32
BF16 = jnp.bfloat16
I32 = jnp.int32

D_MODEL = 2048
HEAD_DIM = 128
N_HEADS = 8
ATTN_WIDTH = N_HEADS * HEAD_DIM
LRU_WIDTH = D_MODEL - ATTN_WIDTH
N_LRU_BLOCKS = 8
LRU_BLOCK = LRU_WIDTH // N_LRU_BLOCKS
CONV_WIDTH = 4
CONV_LEFT = 2
LRU_C = 8.0
DILATIONS = (1, 4, 16)
HALF_WINDOW = 64
ROPE_THETA = 10000.0
N_KEYS = 128
N_EXPERTS = N_KEYS * N_KEYS
PEER_HEADS = 8
PEER_TOPK = 16
PEER_HALF = 128
N_TERMS = PEER_HEADS * PEER_TOPK
EPS = 1e-6
NEG = -1e30

LANES = 128
SUBLANES = 8
VMEM_LIMIT_BYTES = 56 * 1024 * 1024

TM_IN = 256
N_IN_BLOCKS = 5
TP_ATTN = 2048
TQ = 128
TS_LRU = 512
TM_OUT = 256
TM_ROUTE = 512
TM_PEER = 512
EB_PEER = 512
N_I1_CHUNKS = 2
I1_PER_CHUNK = N_KEYS // N_I1_CHUNKS
COEF_PITCH = I1_PER_CHUNK + SUBLANES
COEF_GROUP = 32


def _cparams(sem):
    return pltpu.CompilerParams(dimension_semantics=sem, vmem_limit_bytes=VMEM_LIMIT_BYTES)


def _rms(x, g):
    return x * lax.rsqrt(jnp.mean(x * x, axis=-1, keepdims=True) + EPS) * g


def _gelu(x):
    return 0.5 * x * (1.0 + lax.erf(x * (1.0 / math.sqrt(2.0))))


def _sigmoid(x):
    return 1.0 / (1.0 + jnp.exp(-x))


def _in_proj_kernel(x_ref, g_ref, w_ref, qg_ref, kg_ref, cos_ref, sin_ref,
                    q_ref, k_ref, v_ref, xr_ref, gate_ref):
    hn = _rms(x_ref[...], g_ref[...]).astype(BF16)
    cos = cos_ref[...]
    sin = sin_ref[...]

    def block(j):
        return jnp.dot(hn, w_ref[:, j * ATTN_WIDTH:(j + 1) * ATTN_WIDTH], preferred_element_type=F32)

    def head_norm_rope(y, g):
        outs = []
        for h in range(N_HEADS):
            t = _rms(y[:, h * HEAD_DIM:(h + 1) * HEAD_DIM], g)
            outs.append(t * cos + pltpu.roll(t, HEAD_DIM // 2, 1) * sin)
        return jnp.concatenate(outs, axis=-1)

    q_ref[...] = head_norm_rope(block(0), qg_ref[...])
    k_ref[...] = head_norm_rope(block(1), kg_ref[...])
    v_ref[...] = block(2)
    xr_ref[...] = block(3)
    gate_ref[...] = block(4)


def _in_proj(x, ln1_g, w_in, qg, kg, cos, sin, seq):
    t = x.shape[0]
    tm = min(TM_IN, seq)
    nblk = seq // tm
    col = ATTN_WIDTH
    tok = lambda i: (i, 0)
    const = lambda i: (0, 0)
    pos = lambda i: (i % nblk, 0)
    out = jax.ShapeDtypeStruct((t, col), F32)
    return pl.pallas_call(
        _in_proj_kernel,
        grid=(t // tm,),
        in_specs=[
            pl.BlockSpec((tm, D_MODEL), tok),
            pl.BlockSpec((1, D_MODEL), const),
            pl.BlockSpec((D_MODEL, N_IN_BLOCKS * col), const, pipeline_mode=pl.Buffered(1)),
            pl.BlockSpec((1, HEAD_DIM), const),
            pl.BlockSpec((1, HEAD_DIM), const),
            pl.BlockSpec((tm, HEAD_DIM), pos),
            pl.BlockSpec((tm, HEAD_DIM), pos),
        ],
        out_specs=[pl.BlockSpec((tm, col), tok)] * 5,
        out_shape=[out] * 5,
        compiler_params=_cparams(("arbitrary",)),
        name="in_proj",
    )(x, ln1_g, w_in, qg, kg, cos, sin)


def _attn_kernel(q_ref, kp_ref, kc_ref, kn_ref, vp_ref, vc_ref, vn_ref, o_ref,
                 acc_ref, m_ref, l_ref, *, n_tiles):
    i = pl.program_id(2)
    tp = q_ref.shape[0]
    scale = HEAD_DIM ** -0.5
    qq = lax.broadcasted_iota(I32, (TQ, 2 * TQ), 0)
    kk = lax.broadcasted_iota(I32, (TQ, 2 * TQ), 1)
    band = (kk >= qq) & (kk - qq <= 2 * HALF_WINDOW)
    first = i == 0
    last = i == n_tiles - 1

    for g, d in enumerate(DILATIONS):
        per_res = tp // d
        n_sub = per_res // TQ
        for r in range(d):
            for u in range(n_sub):
                def rows(ref, start, count):
                    if d == 1:
                        return ref[pl.ds(start, count), :]
                    return ref[pl.ds(r + d * start, count, stride=d), :]

                q = rows(q_ref, u * TQ, TQ)
                lo = u * TQ - HALF_WINDOW
                hi = lo + 2 * TQ
                kparts, vparts = [], []
                if lo < 0:
                    kparts.append(rows(kp_ref, per_res + lo, -lo))
                    vparts.append(rows(vp_ref, per_res + lo, -lo))
                c0, c1 = max(lo, 0), min(hi, per_res)
                kparts.append(rows(kc_ref, c0, c1 - c0))
                vparts.append(rows(vc_ref, c0, c1 - c0))
                if hi > per_res:
                    kparts.append(rows(kn_ref, 0, hi - per_res))
                    vparts.append(rows(vn_ref, 0, hi - per_res))
                kb = jnp.concatenate(kparts, axis=0).astype(BF16)
                vb = jnp.concatenate(vparts, axis=0).astype(BF16)
                s = lax.dot_general(q.astype(BF16), kb, (((1,), (1,)), ((), ())),
                                    preferred_element_type=F32) * scale
                ok = band
                if lo < 0:
                    ok = ok & ((kk >= -lo) | jnp.logical_not(first))
                if hi > per_res:
                    ok = ok & ((kk < 2 * TQ - (hi - per_res)) | jnp.logical_not(last))
                s = jnp.where(ok, s, NEG)
                m = jnp.max(s, axis=-1, keepdims=True)
                p = jnp.exp(s - m)
                l = jnp.sum(p, axis=-1, keepdims=True)
                o = jnp.dot(p.astype(BF16), vb, preferred_element_type=F32)
                if d == 1:
                    dst = pl.ds(u * TQ, TQ)
                else:
                    dst = pl.ds(r + d * u * TQ, TQ, stride=d)
                acc_ref[g, dst, :] = o
                m_ref[g, dst, :] = jnp.broadcast_to(m, (TQ, LANES))
                l_ref[g, dst, :] = jnp.broadcast_to(l, (TQ, LANES))

    m_all = jnp.maximum(jnp.maximum(m_ref[0], m_ref[1]), m_ref[2])
    num = jnp.zeros((tp, LANES), F32)
    den = jnp.zeros((tp, LANES), F32)
    for g in range(len(DILATIONS)):
        w = jnp.exp(m_ref[g] - m_all)
        num = num + w * acc_ref[g]
        den = den + w * l_ref[g]
    o_ref[...] = num / den


def _dil_attn(q, k, v):
    b, s, _ = q.shape
    tp = TP_ATTN
    n_tiles = s // tp
    cur = lambda bb, h, i: (bb, i, h)
    prv = lambda bb, h, i: (bb, jnp.maximum(i - 1, 0), h)
    nxt = lambda bb, h, i: (bb, jnp.minimum(i + 1, n_tiles - 1), h)
    blk = (None, tp, HEAD_DIM)
    return pl.pallas_call(
        functools.partial(_attn_kernel, n_tiles=n_tiles),
        grid=(b, N_HEADS, n_tiles),
        in_specs=[
            pl.BlockSpec(blk, cur),
            pl.BlockSpec(blk, prv), pl.BlockSpec(blk, cur), pl.BlockSpec(blk, nxt),
            pl.BlockSpec(blk, prv), pl.BlockSpec(blk, cur), pl.BlockSpec(blk, nxt),
        ],
        out_specs=pl.BlockSpec(blk, cur),
        out_shape=jax.ShapeDtypeStruct((b, s, ATTN_WIDTH), F32),
        scratch_shapes=[pltpu.VMEM((len(DILATIONS), tp, LANES), F32)] * 3,
        compiler_params=_cparams(("arbitrary", "arbitrary", "arbitrary")),
        name="dil_attn",
    )(q, k, k, k, v, v, v)


def _rglru_kernel(xf_p, xf_c, xf_n, xr_p, xr_c, xr_n, cw_ref, cb_ref, w_ref, ba_ref, bx_ref,
                  lam_ref, hf_ref, hr_ref, xe_ref, a_ref, b_ref, h_ref, carry_ref, *, n_tiles):
    i = pl.program_id(1)
    ts = xf_c.shape[0]

    @pl.when(i == 0)
    def _():
        carry_ref[...] = jnp.zeros_like(carry_ref)

    def gates(direction, xp, xc_ref, xn, at_start, at_end):
        xe_ref[0:SUBLANES, :] = jnp.where(at_start, 0.0, xp[...])
        xe_ref[SUBLANES:SUBLANES + ts, :] = xc_ref[...]
        xe_ref[SUBLANES + ts:2 * SUBLANES + ts, :] = jnp.where(at_end, 0.0, xn[...])
        base = SUBLANES - CONV_LEFT
        xc = cb_ref[...] + xe_ref[pl.ds(base, ts), :] * cw_ref[0:1, :]
        for jj in range(1, CONV_WIDTH):
            xc = xc + xe_ref[pl.ds(base + jj, ts), :] * cw_ref[jj:jj + 1, :]
        lam = lam_ref[direction:direction + 1, :]
        z = -lam
        softplus = jnp.maximum(z, 0.0) + jnp.log(1.0 + jnp.exp(-jnp.abs(z)))
        c = -LRU_C * softplus
        for g in range(N_LRU_BLOCKS):
            sl = slice(g * LRU_BLOCK, (g + 1) * LRU_BLOCK)
            xg = xc[:, sl]
            y = jnp.dot(xg.astype(BF16), w_ref[direction, g], preferred_element_type=F32)
            r = _sigmoid(y[:, :LRU_BLOCK] + ba_ref[direction:direction + 1, sl])
            gi = _sigmoid(y[:, LRU_BLOCK:] + bx_ref[direction:direction + 1, sl])
            log_a = c[:, sl] * r
            a = jnp.exp(log_a)
            bt = jnp.sqrt(-jnp.tanh(log_a) * (a * a + 1.0)) * gi * xg
            a_ref[direction, pl.ds(g, ts, stride=N_LRU_BLOCKS), :] = a
            b_ref[direction, pl.ds(g, ts, stride=N_LRU_BLOCKS), :] = bt

    gates(0, xf_p, xf_c, xf_n, i == 0, i == n_tiles - 1)
    gates(1, xr_p, xr_c, xr_n, i == n_tiles - 1, i == 0)

    unroll = 8

    def step(it, carry):
        hf, hr = carry
        for uu in range(unroll):
            tf = pl.multiple_of((it * unroll + uu) * SUBLANES, SUBLANES)
            tr = pl.multiple_of((ts - 1 - it * unroll - uu) * SUBLANES, SUBLANES)
            hf = a_ref[0, pl.ds(tf, SUBLANES), :] * hf + b_ref[0, pl.ds(tf, SUBLANES), :]
            hr = a_ref[1, pl.ds(tr, SUBLANES), :] * hr + b_ref[1, pl.ds(tr, SUBLANES), :]
            h_ref[0, pl.ds(tf, SUBLANES), :] = hf
            h_ref[1, pl.ds(tr, SUBLANES), :] = hr
        return hf, hr

    hf, hr = lax.fori_loop(0, ts // unroll, step, (carry_ref[0], carry_ref[1]))
    carry_ref[0] = hf
    carry_ref[1] = hr
    for g in range(N_LRU_BLOCKS):
        sl = slice(g * LRU_BLOCK, (g + 1) * LRU_BLOCK)
        hf_ref[:, sl] = h_ref[0, pl.ds(g, ts, stride=N_LRU_BLOCKS), :]
        hr_ref[:, sl] = h_ref[1, pl.ds(g, ts, stride=N_LRU_BLOCKS), :]


def _rglru(xr, conv_w, conv_b, w_cat, ba, bx, lam):
    b, s, _ = xr.shape
    ts = TS_LRU
    n_tiles = s // ts
    per = ts // SUBLANES
    n8 = s // SUBLANES
    halo = (None, SUBLANES, LRU_WIDTH)
    tile = (None, ts, LRU_WIDTH)
    rev = lambda i: n_tiles - 1 - i
    const2 = lambda bb, i: (0, 0)
    in_specs = [
        pl.BlockSpec(halo, lambda bb, i: (bb, jnp.maximum(i * per - 1, 0), 0)),
        pl.BlockSpec(tile, lambda bb, i: (bb, i, 0)),
        pl.BlockSpec(halo, lambda bb, i: (bb, jnp.minimum((i + 1) * per, n8 - 1), 0)),
        pl.BlockSpec(halo, lambda bb, i: (bb, jnp.maximum(rev(i) * per - 1, 0), 0)),
        pl.BlockSpec(tile, lambda bb, i: (bb, rev(i), 0)),
        pl.BlockSpec(halo, lambda bb, i: (bb, jnp.minimum((rev(i) + 1) * per, n8 - 1), 0)),
        pl.BlockSpec((CONV_WIDTH, LRU_WIDTH), const2),
        pl.BlockSpec((1, LRU_WIDTH), const2),
        pl.BlockSpec((2, N_LRU_BLOCKS, LRU_BLOCK, 2 * LRU_BLOCK), lambda bb, i: (0, 0, 0, 0)),
        pl.BlockSpec((2, LRU_WIDTH), const2),
        pl.BlockSpec((2, LRU_WIDTH), const2),
        pl.BlockSpec((2, LRU_WIDTH), const2),
    ]
    out = jax.ShapeDtypeStruct((b, s, LRU_WIDTH), F32)
    return pl.pallas_call(
        functools.partial(_rglru_kernel, n_tiles=n_tiles),
        grid=(b, n_tiles),
        in_specs=in_specs,
        out_specs=[pl.BlockSpec(tile, lambda bb, i: (bb, i, 0)),
                   pl.BlockSpec(tile, lambda bb, i: (bb, rev(i), 0))],
        out_shape=[out, out],
        scratch_shapes=[
            pltpu.VMEM((ts + 2 * SUBLANES, LRU_WIDTH), F32),
            pltpu.VMEM((2, ts * N_LRU_BLOCKS, LANES), F32),
            pltpu.VMEM((2, ts * N_LRU_BLOCKS, LANES), F32),
            pltpu.VMEM((2, ts * N_LRU_BLOCKS, LANES), F32),
            pltpu.VMEM((2, SUBLANES, LANES), F32),
        ],
        compiler_params=_cparams(("arbitrary", "arbitrary")),
        name="rglru",
    )(xr, xr, xr, xr, xr, xr, conv_w, conv_b, w_cat, ba, bx, lam)


def _out_proj_kernel(attn_ref, hf_ref, hr_ref, gate_ref, x_ref, ga_ref, gl_ref, w_ref, g2_ref,
                     x1_ref, xn_ref):
    lru = (hf_ref[...] + hr_ref[...]) * _gelu(gate_ref[...])
    na = _rms(attn_ref[...], ga_ref[...]).astype(BF16)
    nl = _rms(lru, gl_ref[...]).astype(BF16)
    mix = jnp.dot(jnp.concatenate([na, nl], axis=-1), w_ref[...], preferred_element_type=F32)
    x1 = x_ref[...] + mix
    x1_ref[...] = x1
    xn_ref[...] = _rms(x1, g2_ref[...]).astype(BF16)


def _out_proj(attn, hf, hr, gate, x, ga, gl, w_out, g2):
    t = x.shape[0]
    tm = TM_OUT
    tok = lambda i: (i, 0)
    const = lambda i: (0, 0)
    half = pl.BlockSpec((tm, ATTN_WIDTH), tok)
    full = pl.BlockSpec((tm, D_MODEL), tok)
    return pl.pallas_call(
        _out_proj_kernel,
        grid=(t // tm,),
        in_specs=[half, half, half, half, full,
                  pl.BlockSpec((1, ATTN_WIDTH), const), pl.BlockSpec((1, LRU_WIDTH), const),
                  pl.BlockSpec((D_MODEL, D_MODEL), const), pl.BlockSpec((1, D_MODEL), const)],
        out_specs=[full, full],
        out_shape=[jax.ShapeDtypeStruct((t, D_MODEL), F32), jax.ShapeDtypeStruct((t, D_MODEL), BF16)],
        compiler_params=_cparams(("arbitrary",)),
        name="out_proj",
    )(attn, hf, hr, gate, x, ga, gl, w_out, g2)


def _top16_rows(x, codes):
    big = jnp.iinfo(jnp.int32).max
    vals, ids = [], []
    for _ in range(PEER_TOPK):
        m = jnp.max(x, axis=0, keepdims=True)
        idx = jnp.min(jnp.where(x == m, codes, big), axis=0, keepdims=True)
        x = jnp.where(codes == idx, -jnp.inf, x)
        vals.append(m)
        ids.append(idx)
    return jnp.concatenate(vals, axis=0), jnp.concatenate(ids, axis=0)


def _sort16_network():
    def merge(lo, hi, r):
        step = r * 2
        if step < hi - lo:
            yield from merge(lo, hi, step)
            yield from merge(lo + r, hi, step)
            for i in range(lo + r, hi - r, step):
                yield (i, i + r)
        else:
            yield (lo, lo + r)

    def sort(lo, hi):
        if hi - lo >= 1:
            mid = lo + (hi - lo) // 2
            yield from sort(lo, mid)
            yield from sort(mid + 1, hi)
            yield from merge(lo, hi, 1)

    return tuple(sort(0, PEER_TOPK - 1))


SORT16 = _sort16_network()


def _top16_of_keys(sc):
    lanes = sc.shape[1]
    sub = lax.broadcasted_iota(I32, (SUBLANES, lanes), 0)
    v = [sc[SUBLANES * j:SUBLANES * (j + 1), :] for j in range(PEER_TOPK)]
    ix = [sub + SUBLANES * j for j in range(PEER_TOPK)]
    for i, j in SORT16:
        keep = v[i] >= v[j]
        v[i], v[j] = jnp.maximum(v[i], v[j]), jnp.minimum(v[i], v[j])
        ix[i], ix[j] = jnp.where(keep, ix[i], ix[j]), jnp.where(keep, ix[j], ix[i])
    big = jnp.iinfo(jnp.int32).max
    vals, ids = [], []
    for a in range(PEER_TOPK):
        m = jnp.max(v[0], axis=0, keepdims=True)
        idx = jnp.min(jnp.where(v[0] == m, ix[0], big), axis=0, keepdims=True)
        vals.append(m)
        ids.append(idx)
        win = ix[0] == idx
        for j in range(PEER_TOPK - 1 - a):
            v[j] = jnp.where(win, v[j + 1], v[j])
            ix[j] = jnp.where(win, ix[j + 1], ix[j])
    return jnp.concatenate(vals, axis=0), jnp.concatenate(ids, axis=0)


def _peer_route_kernel(xn_ref, wq_ref, keys_ref, i1_ref, i2_ref, g_ref, sc_ref, r1_ref, r2_ref, rg_ref):
    tm = xn_ref.shape[0]
    n_q = tm // LANES
    pq = jnp.dot(xn_ref[...], wq_ref[...], preferred_element_type=F32).astype(BF16)
    for hp in range(2 * PEER_HEADS):
        sc = lax.dot_general(keys_ref[hp], pq[:, hp * PEER_HALF:(hp + 1) * PEER_HALF],
                             (((1,), (1,)), ((), ())), preferred_element_type=F32)
        for q in range(n_q):
            sc_ref[hp, q] = sc[:, q * LANES:(q + 1) * LANES]

    row16 = lax.broadcasted_iota(I32, (PEER_TOPK, LANES), 0)
    row8 = lax.broadcasted_iota(I32, (SUBLANES, LANES), 0)
    cand_codes = jnp.concatenate(
        [row16] + [a * PEER_TOPK + row8 for a in range(1, 8)] + [(row8 + 8) * PEER_TOPK], axis=0)

    def one_head(h, carry):
        for q in range(n_q):
            v1, id1 = _top16_of_keys(sc_ref[2 * h, q])
            v2, id2 = _top16_of_keys(sc_ref[2 * h + 1, q])
            cand = jnp.concatenate(
                [v1[0:1] + v2] + [v1[a:a + 1] + v2[0:8] for a in range(1, 8)] + [v1[8:16] + v2[0:1]],
                axis=0)
            top, code = _top16_rows(cand, cand_codes)
            ra = code >> 4
            rb = code & (PEER_TOPK - 1)
            i1 = jnp.zeros((PEER_TOPK, LANES), I32)
            i2 = jnp.zeros((PEER_TOPK, LANES), I32)
            for a in range(PEER_TOPK):
                i1 = jnp.where(ra == a, id1[a:a + 1], i1)
                i2 = jnp.where(rb == a, id2[a:a + 1], i2)
            e = jnp.exp(top - top[0:1])
            g = e / jnp.sum(e, axis=0, keepdims=True)
            rows = pl.ds(pl.multiple_of(h * PEER_TOPK, PEER_TOPK), PEER_TOPK)
            r1_ref[q, rows, :] = i1
            r2_ref[q, rows, :] = i2
            rg_ref[q, rows, :] = g
        return carry

    lax.fori_loop(0, PEER_HEADS, one_head, 0)
    for q in range(n_q):
        tok = slice(q * LANES, (q + 1) * LANES)
        i1_ref[tok, :] = r1_ref[q].T
        i2_ref[tok, :] = r2_ref[q].T
        g_ref[tok, :] = rg_ref[q].T


def _peer_route(xn, wq, keys):
    t = xn.shape[0]
    tm = TM_ROUTE
    n_q = tm // LANES
    tok = lambda i: (i, 0)
    term = pl.BlockSpec((tm, N_TERMS), tok)
    return pl.pallas_call(
        _peer_route_kernel,
        grid=(t // tm,),
        in_specs=[pl.BlockSpec((tm, D_MODEL), tok),
                  pl.BlockSpec((D_MODEL, D_MODEL), lambda i: (0, 0), pipeline_mode=pl.Buffered(1)),
                  pl.BlockSpec((2 * PEER_HEADS, N_KEYS, PEER_HALF), lambda i: (0, 0, 0))],
        out_specs=[term, term, term],
        out_shape=[jax.ShapeDtypeStruct((t, N_TERMS), I32), jax.ShapeDtypeStruct((t, N_TERMS), I32),
                   jax.ShapeDtypeStruct((t, N_TERMS), F32)],
        scratch_shapes=[pltpu.VMEM((2 * PEER_HEADS, n_q, N_KEYS, LANES), F32),
                        pltpu.VMEM((n_q, N_TERMS, LANES), I32),
                        pltpu.VMEM((n_q, N_TERMS, LANES), I32),
                        pltpu.VMEM((n_q, N_TERMS, LANES), F32)],
        compiler_params=_cparams(("arbitrary",)),
        name="peer_route",
    )(xn, wq, keys)


def _peer_dense_kernel(xn_ref, x1_ref, i1_ref, i2_ref, g_ref, ut_ref, v_ref, y_ref, coef_ref):
    e = pl.program_id(1)
    tm = xn_ref.shape[0]
    blocks_per_chunk = (N_EXPERTS // EB_PEER) // N_I1_CHUNKS
    i1_per_block = EB_PEER // N_KEYS

    @pl.when(e == 0)
    def _():
        y_ref[...] = x1_ref[...]

    @pl.when(e % blocks_per_chunk == 0)
    def _():
        i1_base = (e // blocks_per_chunk) * I1_PER_CHUNK
        rows1 = lax.broadcasted_iota(I32, (I1_PER_CHUNK, N_TERMS), 0) + i1_base
        rows2 = lax.broadcasted_iota(I32, (N_KEYS, N_TERMS), 0)

        def token_group(gi, carry):
            t0 = pl.multiple_of(gi * COEF_GROUP, COEF_GROUP)
            i1b = i1_ref[pl.ds(t0, COEF_GROUP), :]
            i2b = i2_ref[pl.ds(t0, COEF_GROUP), :]
            gb = g_ref[pl.ds(t0, COEF_GROUP), :]
            zero = jnp.zeros((I1_PER_CHUNK, N_TERMS), BF16)
            for uu in range(0, COEF_GROUP, 2):
                a1, a2 = [], []
                for v in (uu, uu + 1):
                    a1.append(jnp.where(rows1 == i1b[v:v + 1, :], gb[v:v + 1, :], 0.0).astype(BF16))
                    a2.append(jnp.where(rows2 == i2b[v:v + 1, :], 1.0, 0.0).astype(BF16))
                lhs = jnp.concatenate([jnp.concatenate([a1[0], zero], axis=1),
                                       jnp.concatenate([zero, a1[1]], axis=1)], axis=0)
                rhs = jnp.concatenate(a2, axis=1)
                c = lax.dot_general(lhs, rhs, (((1,), (1,)), ((), ())), preferred_element_type=F32)
                for k, v in enumerate((uu, uu + 1)):
                    row = pl.multiple_of((t0 + v) * COEF_PITCH, SUBLANES)
                    coef_ref[pl.ds(row, I1_PER_CHUNK), :] = c[k * I1_PER_CHUNK:(k + 1) * I1_PER_CHUNK]
            return carry

        lax.fori_loop(0, tm // COEF_GROUP, token_group, 0)

    hid = jnp.dot(xn_ref[...], ut_ref[...], preferred_element_type=F32)
    act = _gelu(hid)
    local = (e % blocks_per_chunk) * i1_per_block
    parts = []
    for jj in range(i1_per_block):
        cf = coef_ref[pl.ds(local + jj, tm, stride=COEF_PITCH), :]
        parts.append((cf * act[:, jj * N_KEYS:(jj + 1) * N_KEYS]).astype(BF16))
    y_ref[...] += jnp.dot(jnp.concatenate(parts, axis=-1), v_ref[...], preferred_element_type=F32)


def _peer_dense(xn, x1, i1, i2, g, ut, v):
    t = xn.shape[0]
    tm = TM_PEER
    tok = lambda i, e: (i, 0)
    term = pl.BlockSpec((tm, N_TERMS), tok)
    return pl.pallas_call(
        _peer_dense_kernel,
        grid=(t // tm, N_EXPERTS // EB_PEER),
        in_specs=[pl.BlockSpec((tm, D_MODEL), tok),
                  pl.BlockSpec((tm, D_MODEL), tok),
                  term, term, term,
                  pl.BlockSpec((D_MODEL, EB_PEER), lambda i, e: (0, e)),
                  pl.BlockSpec((EB_PEER, D_MODEL), lambda i, e: (e, 0))],
        out_specs=pl.BlockSpec((tm, D_MODEL), tok),
        out_shape=jax.ShapeDtypeStruct((t, D_MODEL), F32),
        scratch_shapes=[pltpu.VMEM((tm * COEF_PITCH, LANES), F32)],
        compiler_params=_cparams(("arbitrary", "arbitrary")),
        name="peer_dense",
    )(xn, x1, i1, i2, g, ut, v)


def _rope_tables(seq):
    inv = ROPE_THETA ** (-jnp.arange(0, HEAD_DIM, 2, dtype=F32) / HEAD_DIM)
    ang = jnp.arange(seq, dtype=F32)[:, None] * inv[None, :]
    cos, sin = jnp.cos(ang), jnp.sin(ang)
    return jnp.concatenate([cos, cos], axis=-1), jnp.concatenate([-sin, sin], axis=-1)


def _prep_layer(p):
    (ln1_g, w_in, qg, kg, conv_w, conv_b, wa, ba, wx, bx, lam, ga, gl, w_out, g2, wq, keys, u, v) = p
    w_cat = jnp.concatenate([wa, wx], axis=-1).astype(BF16)
    return dict(
        ln1_g=ln1_g.reshape(1, -1), w_in=w_in.astype(BF16), qg=qg.reshape(1, -1), kg=kg.reshape(1, -1),
        conv_w=conv_w, conv_b=conv_b.reshape(1, -1), w_cat=w_cat, ba=ba, bx=bx, lam=lam,
        ga=ga.reshape(1, -1), gl=gl.reshape(1, -1), w_out=w_out.astype(BF16), g2=g2.reshape(1, -1),
        wq=wq.astype(BF16), keys=keys.reshape(2 * PEER_HEADS, N_KEYS, PEER_HALF).astype(BF16),
        ut=u.astype(BF16).T, v=v.astype(BF16))


def _layer(x, w, cos, sin):
    b, s, _ = x.shape
    xt = x.reshape(b * s, D_MODEL)
    q, k, v, xr, gate = _in_proj(xt, w["ln1_g"], w["w_in"], w["qg"], w["kg"], cos, sin, s)
    shp = (b, s, ATTN_WIDTH)
    attn = _dil_attn(q.reshape(shp), k.reshape(shp), v.reshape(shp))
    hf, hr = _rglru(xr.reshape(shp), w["conv_w"], w["conv_b"], w["w_cat"], w["ba"], w["bx"], w["lam"])
    x1, xn = _out_proj(attn.reshape(b * s, -1), hf.reshape(b * s, -1), hr.reshape(b * s, -1), gate, xt,
                       w["ga"], w["gl"], w["w_out"], w["g2"])
    i1, i2, g = _peer_route(xn, w["wq"], w["keys"])
    y = _peer_dense(xn, x1, i1, i2, g, w["ut"], w["v"])
    return y.reshape(b, s, D_MODEL)


def kernel(x_prompt, x_sample, ln1_g, w_in, q_norm_g, k_norm_g, conv_w, conv_b, lru_wa, lru_ba, lru_wx, lru_bx, lru_lam, attn_out_g, lru_out_g, w_out, ln2_g, peer_wq, peer_keys, peer_u, peer_v):
    params = (ln1_g, w_in, q_norm_g, k_norm_g, conv_w, conv_b, lru_wa, lru_ba, lru_wx, lru_bx,
              lru_lam, attn_out_g, lru_out_g, w_out, ln2_g, peer_wq, peer_keys, peer_u, peer_v)
    depth = ln1_g.shape[0]
    layers = [_prep_layer([p[l] for p in params]) for l in range(depth)]

    def trunk(x):
        cos, sin = _rope_tables(x.shape[1])
        for w in layers:
            x = _layer(x, w, cos, sin)
        return x

    return (trunk(x_prompt), trunk(x_sample))
```

```python
import functools
import math

import jax
import jax.numpy as jnp
from jax import lax
from jax.experimental import pallas as pl
from jax.experimental.pallas import tpu as pltpu

F32 = jnp.float32
BF16 = jnp.bfloat16
I32 = jnp.int32

D_MODEL = 2048
HEAD_DIM = 128
N_HEADS = 8
ATTN_WIDTH = N_HEADS * HEAD_DIM
LRU_WIDTH = D_MODEL - ATTN_WIDTH
N_LRU_BLOCKS = 8
LRU_BLOCK = LRU_WIDTH // N_LRU_BLOCKS
CONV_WIDTH = 4
CONV_LEFT = 2
LRU_C = 8.0
DILATIONS = (1, 4, 16)
HALF_WINDOW = 64
ROPE_THETA = 10000.0
N_KEYS = 128
N_EXPERTS = N_KEYS * N_KEYS
PEER_HEADS = 8
PEER_TOPK = 16
PEER_HALF = 128
N_TERMS = PEER_HEADS * PEER_TOPK
EPS = 1e-6
NEG = -1e30

LANES = 128
SUBLANES = 8
VMEM_LIMIT_BYTES = 56 * 1024 * 1024

TM_IN = 256
N_IN_BLOCKS = 5
TP_ATTN = 2048
TQ = 128
TS_LRU = 512
TM_OUT = 256
TM_PEER = 512
EB_PEER = 512
N_I1_CHUNKS = 2
I1_PER_CHUNK = N_KEYS // N_I1_CHUNKS
COEF_PITCH = I1_PER_CHUNK + SUBLANES
COEF_GROUP = 32


def _cparams(sem):
    return pltpu.CompilerParams(dimension_semantics=sem, vmem_limit_bytes=VMEM_LIMIT_BYTES)


def _rms(x, g):
    return x * lax.rsqrt(jnp.mean(x * x, axis=-1, keepdims=True) + EPS) * g


def _gelu(x):
    return 0.5 * x * (1.0 + lax.erf(x * (1.0 / math.sqrt(2.0))))


def _sigmoid(x):
    return 1.0 / (1.0 + jnp.exp(-x))


def _in_proj_kernel(x_ref, g_ref, w_ref, qg_ref, kg_ref, cos_ref, sin_ref,
                    q_ref, k_ref, v_ref, xr_ref, gate_ref):
    hn = _rms(x_ref[...], g_ref[...]).astype(BF16)
    cos = cos_ref[...]
    sin = sin_ref[...]

    def block(j):
        return jnp.dot(hn, w_ref[:, j * ATTN_WIDTH:(j + 1) * ATTN_WIDTH], preferred_element_type=F32)

    def head_norm_rope(y, g):
        outs = []
        for h in range(N_HEADS):
            t = _rms(y[:, h * HEAD_DIM:(h + 1) * HEAD_DIM], g)
            outs.append(t * cos + pltpu.roll(t, HEAD_DIM // 2, 1) * sin)
        return jnp.concatenate(outs, axis=-1)

    q_ref[...] = head_norm_rope(block(0), qg_ref[...])
    k_ref[...] = head_norm_rope(block(1), kg_ref[...])
    v_ref[...] = block(2)
    xr_ref[...] = block(3)
    gate_ref[...] = block(4)


def _in_proj(x, ln1_g, w_in, qg, kg, cos, sin, seq):
    t = x.shape[0]
    tm = min(TM_IN, seq)
    nblk = seq // tm
    col = ATTN_WIDTH
    tok = lambda i: (i, 0)
    const = lambda i: (0, 0)
    pos = lambda i: (i % nblk, 0)
    out = jax.ShapeDtypeStruct((t, col), F32)
    return pl.pallas_call(
        _in_proj_kernel,
        grid=(t // tm,),
        in_specs=[
            pl.BlockSpec((tm, D_MODEL), tok),
            pl.BlockSpec((1, D_MODEL), const),
            pl.BlockSpec((D_MODEL, N_IN_BLOCKS * col), const, pipeline_mode=pl.Buffered(1)),
            pl.BlockSpec((1, HEAD_DIM), const),
            pl.BlockSpec((1, HEAD_DIM), const),
            pl.BlockSpec((tm, HEAD_DIM), pos),
            pl.BlockSpec((tm, HEAD_DIM), pos),
        ],
        out_specs=[pl.BlockSpec((tm, col), tok)] * 5,
        out_shape=[out] * 5,
        compiler_params=_cparams(("arbitrary",)),
        name="in_proj",
    )(x, ln1_g, w_in, qg, kg, cos, sin)


def _attn_kernel(q_ref, kp_ref, kc_ref, kn_ref, vp_ref, vc_ref, vn_ref, o_ref,
                 acc_ref, m_ref, l_ref, *, n_tiles):
    i = pl.program_id(2)
    tp = q_ref.shape[0]
    scale = HEAD_DIM ** -0.5
    qq = lax.broadcasted_iota(I32, (TQ, 2 * TQ), 0)
    kk = lax.broadcasted_iota(I32, (TQ, 2 * TQ), 1)
    band = (kk >= qq) & (kk - qq <= 2 * HALF_WINDOW)
    first = i == 0
    last = i == n_tiles - 1

    for g, d in enumerate(DILATIONS):
        per_res = tp // d
        n_sub = per_res // TQ
        for r in range(d):
            for u in range(n_sub):
                def rows(ref, start, count):
                    if d == 1:
                        return ref[pl.ds(start, count), :]
                    return ref[pl.ds(r + d * start, count, stride=d), :]

                q = rows(q_ref, u * TQ, TQ)
                lo = u * TQ - HALF_WINDOW
                hi = lo + 2 * TQ
                kparts, vparts = [], []
                if lo < 0:
                    kparts.append(rows(kp_ref, per_res + lo, -lo))
                    vparts.append(rows(vp_ref, per_res + lo, -lo))
                c0, c1 = max(lo, 0), min(hi, per_res)
                kparts.append(rows(kc_ref, c0, c1 - c0))
                vparts.append(rows(vc_ref, c0, c1 - c0))
                if hi > per_res:
                    kparts.append(rows(kn_ref, 0, hi - per_res))
                    vparts.append(rows(vn_ref, 0, hi - per_res))
                kb = jnp.concatenate(kparts, axis=0).astype(BF16)
                vb = jnp.concatenate(vparts, axis=0).astype(BF16)
                s = lax.dot_general(q.astype(BF16), kb, (((1,), (1,)), ((), ())),
                                    preferred_element_type=F32) * scale
                ok = band
                if lo < 0:
                    ok = ok & ((kk >= -lo) | jnp.logical_not(first))
                if hi > per_res:
                    ok = ok & ((kk < 2 * TQ - (hi - per_res)) | jnp.logical_not(last))
                s = jnp.where(ok, s, NEG)
                m = jnp.max(s, axis=-1, keepdims=True)
                p = jnp.exp(s - m)
                l = jnp.sum(p, axis=-1, keepdims=True)
                o = jnp.dot(p.astype(BF16), vb, preferred_element_type=F32)
                if d == 1:
                    dst = pl.ds(u * TQ, TQ)
                else:
                    dst = pl.ds(r + d * u * TQ, TQ, stride=d)
                acc_ref[g, dst, :] = o
                m_ref[g, dst, :] = jnp.broadcast_to(m, (TQ, LANES))
                l_ref[g, dst, :] = jnp.broadcast_to(l, (TQ, LANES))

    m_all = jnp.maximum(jnp.maximum(m_ref[0], m_ref[1]), m_ref[2])
    num = jnp.zeros((tp, LANES), F32)
    den = jnp.zeros((tp, LANES), F32)
    for g in range(len(DILATIONS)):
        w = jnp.exp(m_ref[g] - m_all)
        num = num + w * acc_ref[g]
        den = den + w * l_ref[g]
    o_ref[...] = num / den


def _dil_attn(q, k, v):
    b, s, _ = q.shape
    tp = TP_ATTN
    n_tiles = s // tp
    cur = lambda bb, h, i: (bb, i, h)
    prv = lambda bb, h, i: (bb, jnp.maximum(i - 1, 0), h)
    nxt = lambda bb, h, i: (bb, jnp.minimum(i + 1, n_tiles - 1), h)
    blk = (None, tp, HEAD_DIM)
    return pl.pallas_call(
        functools.partial(_attn_kernel, n_tiles=n_tiles),
        grid=(b, N_HEADS, n_tiles),
        in_specs=[
            pl.BlockSpec(blk, cur),
            pl.BlockSpec(blk, prv), pl.BlockSpec(blk, cur), pl.BlockSpec(blk, nxt),
            pl.BlockSpec(blk, prv), pl.BlockSpec(blk, cur), pl.BlockSpec(blk, nxt),
        ],
        out_specs=pl.BlockSpec(blk, cur),
        out_shape=jax.ShapeDtypeStruct((b, s, ATTN_WIDTH), F32),
        scratch_shapes=[pltpu.VMEM((len(DILATIONS), tp, LANES), F32)] * 3,
        compiler_params=_cparams(("arbitrary", "arbitrary", "arbitrary")),
        name="dil_attn",
    )(q, k, k, k, v, v, v)


def _rglru_kernel(xf_p, xf_c, xf_n, xr_p, xr_c, xr_n, cw_ref, cb_ref, w_ref, ba_ref, bx_ref,
                  lam_ref, hf_ref, hr_ref, xe_ref, a_ref, b_ref, h_ref, carry_ref, *, n_tiles):
    i = pl.program_id(1)
    ts = xf_c.shape[0]

    @pl.when(i == 0)
    def _():
        carry_ref[...] = jnp.zeros_like(carry_ref)

    def gates(direction, xp, xc_ref, xn, at_start, at_end):
        xe_ref[0:SUBLANES, :] = jnp.where(at_start, 0.0, xp[...])
        xe_ref[SUBLANES:SUBLANES + ts, :] = xc_ref[...]
        xe_ref[SUBLANES + ts:2 * SUBLANES + ts, :] = jnp.where(at_end, 0.0, xn[...])
        base = SUBLANES - CONV_LEFT
        xc = cb_ref[...] + xe_ref[pl.ds(base, ts), :] * cw_ref[0:1, :]
        for jj in range(1, CONV_WIDTH):
            xc = xc + xe_ref[pl.ds(base + jj, ts), :] * cw_ref[jj:jj + 1, :]
        lam = lam_ref[direction:direction + 1, :]
        z = -lam
        softplus = jnp.maximum(z, 0.0) + jnp.log(1.0 + jnp.exp(-jnp.abs(z)))
        c = -LRU_C * softplus
        for g in range(N_LRU_BLOCKS):
            sl = slice(g * LRU_BLOCK, (g + 1) * LRU_BLOCK)
            xg = xc[:, sl]
            y = jnp.dot(xg.astype(BF16), w_ref[direction, g], preferred_element_type=F32)
            r = _sigmoid(y[:, :LRU_BLOCK] + ba_ref[direction:direction + 1, sl])
            gi = _sigmoid(y[:, LRU_BLOCK:] + bx_ref[direction:direction + 1, sl])
            log_a = c[:, sl] * r
            a = jnp.exp(log_a)
            bt = jnp.sqrt(-jnp.tanh(log_a) * (a * a + 1.0)) * gi * xg
            a_ref[direction, pl.ds(g, ts, stride=N_LRU_BLOCKS), :] = a
            b_ref[direction, pl.ds(g, ts, stride=N_LRU_BLOCKS), :] = bt

    gates(0, xf_p, xf_c, xf_n, i == 0, i == n_tiles - 1)
    gates(1, xr_p, xr_c, xr_n, i == n_tiles - 1, i == 0)

    unroll = 8

    def step(it, carry):
        hf, hr = carry
        for uu in range(unroll):
            tf = pl.multiple_of((it * unroll + uu) * SUBLANES, SUBLANES)
            tr = pl.multiple_of((ts - 1 - it * unroll - uu) * SUBLANES, SUBLANES)
            hf = a_ref[0, pl.ds(tf, SUBLANES), :] * hf + b_ref[0, pl.ds(tf, SUBLANES), :]
            hr = a_ref[1, pl.ds(tr, SUBLANES), :] * hr + b_ref[1, pl.ds(tr, SUBLANES), :]
            h_ref[0, pl.ds(tf, SUBLANES), :] = hf
            h_ref[1, pl.ds(tr, SUBLANES), :] = hr
        return hf, hr

    hf, hr = lax.fori_loop(0, ts // unroll, step, (carry_ref[0], carry_ref[1]))
    carry_ref[0] = hf
    carry_ref[1] = hr
    for g in range(N_LRU_BLOCKS):
        sl = slice(g * LRU_BLOCK, (g + 1) * LRU_BLOCK)
        hf_ref[:, sl] = h_ref[0, pl.ds(g, ts, stride=N_LRU_BLOCKS), :]
        hr_ref[:, sl] = h_ref[1, pl.ds(g, ts, stride=N_LRU_BLOCKS), :]


def _rglru(xr, conv_w, conv_b, w_cat, ba, bx, lam):
    b, s, _ = xr.shape
    ts = TS_LRU
    n_tiles = s // ts
    per = ts // SUBLANES
    n8 = s // SUBLANES
    halo = (None, SUBLANES, LRU_WIDTH)
    tile = (None, ts, LRU_WIDTH)
    rev = lambda i: n_tiles - 1 - i
    const2 = lambda bb, i: (0, 0)
    in_specs = [
        pl.BlockSpec(halo, lambda bb, i: (bb, jnp.maximum(i * per - 1, 0), 0)),
        pl.BlockSpec(tile, lambda bb, i: (bb, i, 0)),
        pl.BlockSpec(halo, lambda bb, i: (bb, jnp.minimum((i + 1) * per, n8 - 1), 0)),
        pl.BlockSpec(halo, lambda bb, i: (bb, jnp.maximum(rev(i) * per - 1, 0), 0)),
        pl.BlockSpec(tile, lambda bb, i: (bb, rev(i), 0)),
        pl.BlockSpec(halo, lambda bb, i: (bb, jnp.minimum((rev(i) + 1) * per, n8 - 1), 0)),
        pl.BlockSpec((CONV_WIDTH, LRU_WIDTH), const2),
        pl.BlockSpec((1, LRU_WIDTH), const2),
        pl.BlockSpec((2, N_LRU_BLOCKS, LRU_BLOCK, 2 * LRU_BLOCK), lambda bb, i: (0, 0, 0, 0)),
        pl.BlockSpec((2, LRU_WIDTH), const2),
        pl.BlockSpec((2, LRU_WIDTH), const2),
        pl.BlockSpec((2, LRU_WIDTH), const2),
    ]
    out = jax.ShapeDtypeStruct((b, s, LRU_WIDTH), F32)
    return pl.pallas_call(
        functools.partial(_rglru_kernel, n_tiles=n_tiles),
        grid=(b, n_tiles),
        in_specs=in_specs,
        out_specs=[pl.BlockSpec(tile, lambda bb, i: (bb, i, 0)),
                   pl.BlockSpec(tile, lambda bb, i: (bb, rev(i), 0))],
        out_shape=[out, out],
        scratch_shapes=[
            pltpu.VMEM((ts + 2 * SUBLANES, LRU_WIDTH), F32),
            pltpu.VMEM((2, ts * N_LRU_BLOCKS, LANES), F32),
            pltpu.VMEM((2, ts * N_LRU_BLOCKS, LANES), F32),
            pltpu.VMEM((2, ts * N_LRU_BLOCKS, LANES), F32),
            pltpu.VMEM((2, SUBLANES, LANES), F32),
        ],
        compiler_params=_cparams(("arbitrary", "arbitrary")),
        name="rglru",
    )(xr, xr, xr, xr, xr, xr, conv_w, conv_b, w_cat, ba, bx, lam)


def _out_proj_kernel(attn_ref, hf_ref, hr_ref, gate_ref, x_ref, ga_ref, gl_ref, w_ref, g2_ref, wq_ref,
                     x1_ref, xn_ref, pq_ref):
    lru = (hf_ref[...] + hr_ref[...]) * _gelu(gate_ref[...])
    na = _rms(attn_ref[...], ga_ref[...]).astype(BF16)
    nl = _rms(lru, gl_ref[...]).astype(BF16)
    mix = jnp.dot(jnp.concatenate([na, nl], axis=-1), w_ref[...], preferred_element_type=F32)
    x1 = x_ref[...] + mix
    x1_ref[...] = x1
    xn = _rms(x1, g2_ref[...]).astype(BF16)
    xn_ref[...] = xn
    pq_ref[...] = jnp.dot(xn, wq_ref[...], preferred_element_type=F32).astype(BF16)


def _out_proj(attn, hf, hr, gate, x, ga, gl, w_out, g2, wq):
    t = x.shape[0]
    tm = TM_OUT
    tok = lambda i: (i, 0)
    const = lambda i: (0, 0)
    half = pl.BlockSpec((tm, ATTN_WIDTH), tok)
    full = pl.BlockSpec((tm, D_MODEL), tok)
    weight = pl.BlockSpec((D_MODEL, D_MODEL), const, pipeline_mode=pl.Buffered(1))
    act = jax.ShapeDtypeStruct((t, D_MODEL), BF16)
    return pl.pallas_call(
        _out_proj_kernel,
        grid=(t // tm,),
        in_specs=[half, half, half, half, full,
                  pl.BlockSpec((1, ATTN_WIDTH), const), pl.BlockSpec((1, LRU_WIDTH), const),
                  weight, pl.BlockSpec((1, D_MODEL), const), weight],
        out_specs=[full, full, full],
        out_shape=[jax.ShapeDtypeStruct((t, D_MODEL), F32), act, act],
        compiler_params=_cparams(("arbitrary",)),
        name="out_proj",
    )(attn, hf, hr, gate, x, ga, gl, w_out, g2, wq)


def _top16_rows(x, codes):
    big = jnp.iinfo(jnp.int32).max
    vals, ids = [], []
    for _ in range(PEER_TOPK):
        m = jnp.max(x, axis=0, keepdims=True)
        idx = jnp.min(jnp.where(x == m, codes, big), axis=0, keepdims=True)
        x = jnp.where(codes == idx, -jnp.inf, x)
        vals.append(m)
        ids.append(idx)
    return jnp.concatenate(vals, axis=0), jnp.concatenate(ids, axis=0)


def _sort16_network():
    def merge(lo, hi, r):
        step = r * 2
        if step < hi - lo:
            yield from merge(lo, hi, step)
            yield from merge(lo + r, hi, step)
            for i in range(lo + r, hi - r, step):
                yield (i, i + r)
        else:
            yield (lo, lo + r)

    def sort(lo, hi):
        if hi - lo >= 1:
            mid = lo + (hi - lo) // 2
            yield from sort(lo, mid)
            yield from sort(mid + 1, hi)
            yield from merge(lo, hi, 1)

    return tuple(sort(0, PEER_TOPK - 1))


SORT16 = _sort16_network()


def _top16_of_keys(sc):
    lanes = sc.shape[1]
    sub = lax.broadcasted_iota(I32, (SUBLANES, lanes), 0)
    v = [sc[SUBLANES * j:SUBLANES * (j + 1), :] for j in range(PEER_TOPK)]
    ix = [sub + SUBLANES * j for j in range(PEER_TOPK)]
    for i, j in SORT16:
        keep = v[i] >= v[j]
        v[i], v[j] = jnp.maximum(v[i], v[j]), jnp.minimum(v[i], v[j])
        ix[i], ix[j] = jnp.where(keep, ix[i], ix[j]), jnp.where(keep, ix[j], ix[i])
    big = jnp.iinfo(jnp.int32).max
    vals, ids = [], []
    for a in range(PEER_TOPK):
        m = jnp.max(v[0], axis=0, keepdims=True)
        idx = jnp.min(jnp.where(v[0] == m, ix[0], big), axis=0, keepdims=True)
        vals.append(m)
        ids.append(idx)
        win = ix[0] == idx
        for j in range(PEER_TOPK - 1 - a):
            v[j] = jnp.where(win, v[j + 1], v[j])
            ix[j] = jnp.where(win, ix[j + 1], ix[j])
    return jnp.concatenate(vals, axis=0), jnp.concatenate(ids, axis=0)


def _route_tile(s1, s2, cand_codes):
    v1, id1 = _top16_of_keys(s1)
    v2, id2 = _top16_of_keys(s2)
    cand = jnp.concatenate(
        [v1[0:1] + v2] + [v1[a:a + 1] + v2[0:8] for a in range(1, 8)] + [v1[8:16] + v2[0:1]], axis=0)
    top, code = _top16_rows(cand, cand_codes)
    ra = code >> 4
    rb = code & (PEER_TOPK - 1)
    i1 = jnp.zeros(ra.shape, I32)
    i2 = jnp.zeros(ra.shape, I32)
    for a in range(PEER_TOPK):
        i1 = jnp.where(ra == a, id1[a:a + 1], i1)
        i2 = jnp.where(rb == a, id2[a:a + 1], i2)
    ex = jnp.exp(top - top[0:1])
    return i1, i2, ex / jnp.sum(ex, axis=0, keepdims=True)


def _peer_kernel(xn_ref, x1_ref, pq_ref, keys_ref, ut_ref, v_ref, y_ref,
                 coef_ref, rt1_ref, rt2_ref, rtg_ref, tk1_ref, tk2_ref, tkg_ref):
    i = pl.program_id(0)
    e = pl.program_id(1)
    tm = xn_ref.shape[0]
    n_q = tm // LANES
    blocks_per_chunk = (N_EXPERTS // EB_PEER) // N_I1_CHUNKS
    i1_per_block = EB_PEER // N_KEYS

    def route_part():
        h = e // n_q
        q = e % n_q
        row16 = lax.broadcasted_iota(I32, (PEER_TOPK, LANES), 0)
        row8 = lax.broadcasted_iota(I32, (SUBLANES, LANES), 0)
        cand_codes = jnp.concatenate(
            [row16] + [a * PEER_TOPK + row8 for a in range(1, 8)] + [(row8 + 8) * PEER_TOPK], axis=0)
        pq = pq_ref[...]
        nt = (((1,), (1,)), ((), ()))
        s1 = lax.dot_general(keys_ref[2 * h], pq[:, :PEER_HALF], nt, preferred_element_type=F32)
        s2 = lax.dot_general(keys_ref[2 * h + 1], pq[:, PEER_HALF:], nt, preferred_element_type=F32)
        i1, i2, g = _route_tile(s1, s2, cand_codes)
        rows = pl.ds(pl.multiple_of(h * PEER_TOPK, PEER_TOPK), PEER_TOPK)
        rt1_ref[q, rows, :] = i1
        rt2_ref[q, rows, :] = i2
        rtg_ref[q, rows, :] = g

    def dense_part():
        hid = jnp.dot(xn_ref[...], ut_ref[...], preferred_element_type=F32)
        act = _gelu(hid)
        local = (e % blocks_per_chunk) * i1_per_block
        parts = []
        for jj in range(i1_per_block):
            cf = coef_ref[pl.ds(local + jj, tm, stride=COEF_PITCH), :]
            parts.append((cf * act[:, jj * N_KEYS:(jj + 1) * N_KEYS]).astype(BF16))
        y_ref[...] += jnp.dot(jnp.concatenate(parts, axis=-1), v_ref[...], preferred_element_type=F32)

    @pl.when(i == 0)
    def _():
        route_part()

    @pl.when(i > 0)
    def _():
        @pl.when(e == 0)
        def _():
            y_ref[...] = x1_ref[...]
            for q in range(n_q):
                tok = slice(q * LANES, (q + 1) * LANES)
                tk1_ref[tok, :] = rt1_ref[q].T
                tk2_ref[tok, :] = rt2_ref[q].T
                tkg_ref[tok, :] = rtg_ref[q].T

        @pl.when(e % blocks_per_chunk == 0)
        def _():
            i1_base = (e // blocks_per_chunk) * I1_PER_CHUNK
            rows1 = lax.broadcasted_iota(I32, (I1_PER_CHUNK, N_TERMS), 0) + i1_base
            rows2 = lax.broadcasted_iota(I32, (N_KEYS, N_TERMS), 0)

            def token_group(gi, carry):
                t0 = pl.multiple_of(gi * COEF_GROUP, COEF_GROUP)
                i1b = tk1_ref[pl.ds(t0, COEF_GROUP), :]
                i2b = tk2_ref[pl.ds(t0, COEF_GROUP), :]
                gb = tkg_ref[pl.ds(t0, COEF_GROUP), :]
                zero = jnp.zeros((I1_PER_CHUNK, N_TERMS), BF16)
                for uu in range(0, COEF_GROUP, 2):
                    a1, a2 = [], []
                    for v in (uu, uu + 1):
                        a1.append(jnp.where(rows1 == i1b[v:v + 1, :], gb[v:v + 1, :], 0.0).astype(BF16))
                        a2.append(jnp.where(rows2 == i2b[v:v + 1, :], 1.0, 0.0).astype(BF16))
                    lhs = jnp.concatenate([jnp.concatenate([a1[0], zero], axis=1),
                                           jnp.concatenate([zero, a1[1]], axis=1)], axis=0)
                    rhs = jnp.concatenate(a2, axis=1)
                    c = lax.dot_general(lhs, rhs, (((1,), (1,)), ((), ())), preferred_element_type=F32)
                    for k, v in enumerate((uu, uu + 1)):
                        row = pl.multiple_of((t0 + v) * COEF_PITCH, SUBLANES)
                        coef_ref[pl.ds(row, I1_PER_CHUNK), :] = c[k * I1_PER_CHUNK:(k + 1) * I1_PER_CHUNK]
                return carry

            lax.fori_loop(0, tm // COEF_GROUP, token_group, 0)

        route_part()
        dense_part()


def _peer(xn, x1, pq, keys, ut, v):
    t = xn.shape[0]
    tm = TM_PEER
    n_tiles = t // tm
    n_q = tm // LANES
    n_blocks = N_EXPERTS // EB_PEER
    assert n_blocks == PEER_HEADS * n_q
    prev = lambda i, e: (jnp.maximum(i - 1, 0), 0)
    blk = lambda i, e: jnp.where(i == 0, 0, e)
    return pl.pallas_call(
        _peer_kernel,
        grid=(n_tiles + 1, n_blocks),
        in_specs=[pl.BlockSpec((tm, D_MODEL), prev),
                  pl.BlockSpec((tm, D_MODEL), prev),
                  pl.BlockSpec((LANES, 2 * PEER_HALF),
                               lambda i, e: (jnp.minimum(i, n_tiles - 1) * n_q + e % n_q, e // n_q)),
                  pl.BlockSpec((2 * PEER_HEADS, N_KEYS, PEER_HALF), lambda i, e: (0, 0, 0)),
                  pl.BlockSpec((D_MODEL, EB_PEER), lambda i, e: (0, blk(i, e))),
                  pl.BlockSpec((EB_PEER, D_MODEL), lambda i, e: (blk(i, e), 0))],
        out_specs=pl.BlockSpec((tm, D_MODEL), prev),
        out_shape=jax.ShapeDtypeStruct((t, D_MODEL), F32),
        scratch_shapes=[pltpu.VMEM((tm * COEF_PITCH, LANES), F32),
                        pltpu.VMEM((n_q, N_TERMS, LANES), I32),
                        pltpu.VMEM((n_q, N_TERMS, LANES), I32),
                        pltpu.VMEM((n_q, N_TERMS, LANES), F32),
                        pltpu.VMEM((tm, N_TERMS), I32),
                        pltpu.VMEM((tm, N_TERMS), I32),
                        pltpu.VMEM((tm, N_TERMS), F32)],
        compiler_params=_cparams(("arbitrary", "arbitrary")),
        name="peer",
    )(xn, x1, pq, keys, ut, v)


def _rope_tables(seq):
    inv = ROPE_THETA ** (-jnp.arange(0, HEAD_DIM, 2, dtype=F32) / HEAD_DIM)
    ang = jnp.arange(seq, dtype=F32)[:, None] * inv[None, :]
    cos, sin = jnp.cos(ang), jnp.sin(ang)
    return jnp.concatenate([cos, cos], axis=-1), jnp.concatenate([-sin, sin], axis=-1)


def _prep_layer(p):
    (ln1_g, w_in, qg, kg, conv_w, conv_b, wa, ba, wx, bx, lam, ga, gl, w_out, g2, wq, keys, u, v) = p
    w_cat = jnp.concatenate([wa, wx], axis=-1).astype(BF16)
    return dict(
        ln1_g=ln1_g.reshape(1, -1), w_in=w_in.astype(BF16), qg=qg.reshape(1, -1), kg=kg.reshape(1, -1),
        conv_w=conv_w, conv_b=conv_b.reshape(1, -1), w_cat=w_cat, ba=ba, bx=bx, lam=lam,
        ga=ga.reshape(1, -1), gl=gl.reshape(1, -1), w_out=w_out.astype(BF16), g2=g2.reshape(1, -1),
        wq=wq.astype(BF16), keys=keys.reshape(2 * PEER_HEADS, N_KEYS, PEER_HALF).astype(BF16),
        ut=u.astype(BF16).T, v=v.astype(BF16))


def _layer(x, w, cos, sin):
    b, s, _ = x.shape
    xt = x.reshape(b * s, D_MODEL)
    q, k, v, xr, gate = _in_proj(xt, w["ln1_g"], w["w_in"], w["qg"], w["kg"], cos, sin, s)
    shp = (b, s, ATTN_WIDTH)
    attn = _dil_attn(q.reshape(shp), k.reshape(shp), v.reshape(shp))
    hf, hr = _rglru(xr.reshape(shp), w["conv_w"], w["conv_b"], w["w_cat"], w["ba"], w["bx"], w["lam"])
    x1, xn, pq = _out_proj(attn.reshape(b * s, -1), hf.reshape(b * s, -1), hr.reshape(b * s, -1), gate, xt,
                           w["ga"], w["gl"], w["w_out"], w["g2"], w["wq"])
    y = _peer(xn, x1, pq, w["keys"], w["ut"], w["v"])
    return y.reshape(b, s, D_MODEL)


def kernel(x_prompt, x_sample, ln1_g, w_in, q_norm_g, k_norm_g, conv_w, conv_b, lru_wa, lru_ba, lru_wx, lru_bx, lru_lam, attn_out_g, lru_out_g, w_out, ln2_g, peer_wq, peer_keys, peer_u, peer_v):
    params = (ln1_g, w_in, q_norm_g, k_norm_g, conv_w, conv_b, lru_wa, lru_ba, lru_wx, lru_bx,
              lru_lam, attn_out_g, lru_out_g, w_out, ln2_g, peer_wq, peer_keys, peer_u, peer_v)
    depth = ln1_g.shape[0]
    layers = [_prep_layer([p[l] for p in params]) for l in range(depth)]

    def trunk(x):
        cos, sin = _rope_tables(x.shape[1])
        for w in layers:
            x = _layer(x, w, cos, sin)
        return x

    return (trunk(x_prompt), trunk(x_sample))
```

```python
import functools
import math

import jax
import jax.numpy as jnp
from jax import lax
from jax.experimental import pallas as pl
from jax.experimental.pallas import tpu as pltpu

F32 = jnp.float32
BF16 = jnp.bfloat16
I32 = jnp.int32

D_MODEL = 2048
HEAD_DIM = 128
N_HEADS = 8
ATTN_WIDTH = N_HEADS * HEAD_DIM
LRU_WIDTH = D_MODEL - ATTN_WIDTH
N_LRU_BLOCKS = 8
LRU_BLOCK = LRU_WIDTH // N_LRU_BLOCKS
CONV_WIDTH = 4
CONV_LEFT = 2
LRU_C = 8.0
DILATIONS = (1, 4, 16)
HALF_WINDOW = 64
ROPE_THETA = 10000.0
N_KEYS = 128
N_EXPERTS = N_KEYS * N_KEYS
PEER_HEADS = 8
PEER_TOPK = 16
PEER_HALF = 128
N_TERMS = PEER_HEADS * PEER_TOPK
EPS = 1e-6
NEG = -1e30

LANES = 128
SUBLANES = 8
VMEM_LIMIT_BYTES = 56 * 1024 * 1024

TM_IN = 256
N_IN_BLOCKS = 5
TP_ATTN = 2048
TQ = 128
TS_LRU = 512
TM_OUT = 256
TM_PEER = 512
EB_PEER = 1024
DENSE_PASSES = 2
COEF_ROWS = N_KEYS // 2
COEF_PITCH = COEF_ROWS + SUBLANES
COEF_GROUP = 32


def _cparams(sem):
    return pltpu.CompilerParams(dimension_semantics=sem, vmem_limit_bytes=VMEM_LIMIT_BYTES)


def _rms(x, g):
    return x * lax.rsqrt(jnp.mean(x * x, axis=-1, keepdims=True) + EPS) * g


def _gelu(x):
    return 0.5 * x * (1.0 + lax.erf(x * (1.0 / math.sqrt(2.0))))


def _sigmoid(x):
    return 1.0 / (1.0 + jnp.exp(-x))


def _in_proj_kernel(x_ref, g_ref, w_ref, qg_ref, kg_ref, cos_ref, sin_ref,
                    q_ref, k_ref, v_ref, xr_ref, gate_ref):
    hn = _rms(x_ref[...], g_ref[...]).astype(BF16)
    cos = cos_ref[...]
    sin = sin_ref[...]

    def block(j):
        return jnp.dot(hn, w_ref[:, j * ATTN_WIDTH:(j + 1) * ATTN_WIDTH], preferred_element_type=F32)

    def head_norm_rope(y, g):
        outs = []
        for h in range(N_HEADS):
            t = _rms(y[:, h * HEAD_DIM:(h + 1) * HEAD_DIM], g)
            outs.append(t * cos + pltpu.roll(t, HEAD_DIM // 2, 1) * sin)
        return jnp.concatenate(outs, axis=-1)

    q_ref[...] = head_norm_rope(block(0), qg_ref[...])
    k_ref[...] = head_norm_rope(block(1), kg_ref[...])
    v_ref[...] = block(2)
    xr_ref[...] = block(3)
    gate_ref[...] = block(4)


def _in_proj(x, ln1_g, w_in, qg, kg, cos, sin, seq):
    t = x.shape[0]
    tm = min(TM_IN, seq)
    nblk = seq // tm
    col = ATTN_WIDTH
    tok = lambda i: (i, 0)
    const = lambda i: (0, 0)
    pos = lambda i: (i % nblk, 0)
    out = jax.ShapeDtypeStruct((t, col), F32)
    return pl.pallas_call(
        _in_proj_kernel,
        grid=(t // tm,),
        in_specs=[
            pl.BlockSpec((tm, D_MODEL), tok),
            pl.BlockSpec((1, D_MODEL), const),
            pl.BlockSpec((D_MODEL, N_IN_BLOCKS * col), const, pipeline_mode=pl.Buffered(1)),
            pl.BlockSpec((1, HEAD_DIM), const),
            pl.BlockSpec((1, HEAD_DIM), const),
            pl.BlockSpec((tm, HEAD_DIM), pos),
            pl.BlockSpec((tm, HEAD_DIM), pos),
        ],
        out_specs=[pl.BlockSpec((tm, col), tok)] * 5,
        out_shape=[out] * 5,
        compiler_params=_cparams(("arbitrary",)),
        name="in_proj",
    )(x, ln1_g, w_in, qg, kg, cos, sin)


def _attn_kernel(q_ref, kp_ref, kc_ref, kn_ref, vp_ref, vc_ref, vn_ref, o_ref,
                 acc_ref, m_ref, l_ref, *, n_tiles):
    i = pl.program_id(2)
    tp = q_ref.shape[0]
    scale = HEAD_DIM ** -0.5
    qq = lax.broadcasted_iota(I32, (TQ, 2 * TQ), 0)
    kk = lax.broadcasted_iota(I32, (TQ, 2 * TQ), 1)
    band = (kk >= qq) & (kk - qq <= 2 * HALF_WINDOW)
    first = i == 0
    last = i == n_tiles - 1

    for g, d in enumerate(DILATIONS):
        per_res = tp // d
        n_sub = per_res // TQ
        for r in range(d):
            for u in range(n_sub):
                def rows(ref, start, count):
                    if d == 1:
                        return ref[pl.ds(start, count), :]
                    return ref[pl.ds(r + d * start, count, stride=d), :]

                q = rows(q_ref, u * TQ, TQ)
                lo = u * TQ - HALF_WINDOW
                hi = lo + 2 * TQ
                kparts, vparts = [], []
                if lo < 0:
                    kparts.append(rows(kp_ref, per_res + lo, -lo))
                    vparts.append(rows(vp_ref, per_res + lo, -lo))
                c0, c1 = max(lo, 0), min(hi, per_res)
                kparts.append(rows(kc_ref, c0, c1 - c0))
                vparts.append(rows(vc_ref, c0, c1 - c0))
                if hi > per_res:
                    kparts.append(rows(kn_ref, 0, hi - per_res))
                    vparts.append(rows(vn_ref, 0, hi - per_res))
                kb = jnp.concatenate(kparts, axis=0).astype(BF16)
                vb = jnp.concatenate(vparts, axis=0).astype(BF16)
                s = lax.dot_general(q.astype(BF16), kb, (((1,), (1,)), ((), ())),
                                    preferred_element_type=F32) * scale
                ok = band
                if lo < 0:
                    ok = ok & ((kk >= -lo) | jnp.logical_not(first))
                if hi > per_res:
                    ok = ok & ((kk < 2 * TQ - (hi - per_res)) | jnp.logical_not(last))
                s = jnp.where(ok, s, NEG)
                m = jnp.max(s, axis=-1, keepdims=True)
                p = jnp.exp(s - m)
                l = jnp.sum(p, axis=-1, keepdims=True)
                o = jnp.dot(p.astype(BF16), vb, preferred_element_type=F32)
                if d == 1:
                    dst = pl.ds(u * TQ, TQ)
                else:
                    dst = pl.ds(r + d * u * TQ, TQ, stride=d)
                acc_ref[g, dst, :] = o
                m_ref[g, dst, :] = jnp.broadcast_to(m, (TQ, LANES))
                l_ref[g, dst, :] = jnp.broadcast_to(l, (TQ, LANES))

    m_all = jnp.maximum(jnp.maximum(m_ref[0], m_ref[1]), m_ref[2])
    num = jnp.zeros((tp, LANES), F32)
    den = jnp.zeros((tp, LANES), F32)
    for g in range(len(DILATIONS)):
        w = jnp.exp(m_ref[g] - m_all)
        num = num + w * acc_ref[g]
        den = den + w * l_ref[g]
    o_ref[...] = num / den


def _dil_attn(q, k, v):
    b, s, _ = q.shape
    tp = TP_ATTN
    n_tiles = s // tp
    cur = lambda bb, h, i: (bb, i, h)
    prv = lambda bb, h, i: (bb, jnp.maximum(i - 1, 0), h)
    nxt = lambda bb, h, i: (bb, jnp.minimum(i + 1, n_tiles - 1), h)
    blk = (None, tp, HEAD_DIM)
    return pl.pallas_call(
        functools.partial(_attn_kernel, n_tiles=n_tiles),
        grid=(b, N_HEADS, n_tiles),
        in_specs=[
            pl.BlockSpec(blk, cur),
            pl.BlockSpec(blk, prv), pl.BlockSpec(blk, cur), pl.BlockSpec(blk, nxt),
            pl.BlockSpec(blk, prv), pl.BlockSpec(blk, cur), pl.BlockSpec(blk, nxt),
        ],
        out_specs=pl.BlockSpec(blk, cur),
        out_shape=jax.ShapeDtypeStruct((b, s, ATTN_WIDTH), F32),
        scratch_shapes=[pltpu.VMEM((len(DILATIONS), tp, LANES), F32)] * 3,
        compiler_params=_cparams(("arbitrary", "arbitrary", "arbitrary")),
        name="dil_attn",
    )(q, k, k, k, v, v, v)


def _rglru_kernel(xf_p, xf_c, xf_n, xr_p, xr_c, xr_n, cw_ref, cb_ref, w_ref, ba_ref, bx_ref,
                  lam_ref, hf_ref, hr_ref, xe_ref, a_ref, b_ref, h_ref, carry_ref, *, n_tiles):
    i = pl.program_id(1)
    ts = xf_c.shape[0]

    @pl.when(i == 0)
    def _():
        carry_ref[...] = jnp.zeros_like(carry_ref)

    def gates(direction, xp, xc_ref, xn, at_start, at_end):
        xe_ref[0:SUBLANES, :] = jnp.where(at_start, 0.0, xp[...])
        xe_ref[SUBLANES:SUBLANES + ts, :] = xc_ref[...]
        xe_ref[SUBLANES + ts:2 * SUBLANES + ts, :] = jnp.where(at_end, 0.0, xn[...])
        base = SUBLANES - CONV_LEFT
        xc = cb_ref[...] + xe_ref[pl.ds(base, ts), :] * cw_ref[0:1, :]
        for jj in range(1, CONV_WIDTH):
            xc = xc + xe_ref[pl.ds(base + jj, ts), :] * cw_ref[jj:jj + 1, :]
        lam = lam_ref[direction:direction + 1, :]
        z = -lam
        softplus = jnp.maximum(z, 0.0) + jnp.log(1.0 + jnp.exp(-jnp.abs(z)))
        c = -LRU_C * softplus
        for g in range(N_LRU_BLOCKS):
            sl = slice(g * LRU_BLOCK, (g + 1) * LRU_BLOCK)
            xg = xc[:, sl]
            y = jnp.dot(xg.astype(BF16), w_ref[direction, g], preferred_element_type=F32)
            r = _sigmoid(y[:, :LRU_BLOCK] + ba_ref[direction:direction + 1, sl])
            gi = _sigmoid(y[:, LRU_BLOCK:] + bx_ref[direction:direction + 1, sl])
            log_a = c[:, sl] * r
            a = jnp.exp(log_a)
            bt = jnp.sqrt(-jnp.tanh(log_a) * (a * a + 1.0)) * gi * xg
            a_ref[direction, pl.ds(g, ts, stride=N_LRU_BLOCKS), :] = a
            b_ref[direction, pl.ds(g, ts, stride=N_LRU_BLOCKS), :] = bt

    gates(0, xf_p, xf_c, xf_n, i == 0, i == n_tiles - 1)
    gates(1, xr_p, xr_c, xr_n, i == n_tiles - 1, i == 0)

    unroll = 8

    def step(it, carry):
        hf, hr = carry
        for uu in range(unroll):
            tf = pl.multiple_of((it * unroll + uu) * SUBLANES, SUBLANES)
            tr = pl.multiple_of((ts - 1 - it * unroll - uu) * SUBLANES, SUBLANES)
            hf = a_ref[0, pl.ds(tf, SUBLANES), :] * hf + b_ref[0, pl.ds(tf, SUBLANES), :]
            hr = a_ref[1, pl.ds(tr, SUBLANES), :] * hr + b_ref[1, pl.ds(tr, SUBLANES), :]
            h_ref[0, pl.ds(tf, SUBLANES), :] = hf
            h_ref[1, pl.ds(tr, SUBLANES), :] = hr
        return hf, hr

    hf, hr = lax.fori_loop(0, ts // unroll, step, (carry_ref[0], carry_ref[1]))
    carry_ref[0] = hf
    carry_ref[1] = hr
    for g in range(N_LRU_BLOCKS):
        sl = slice(g * LRU_BLOCK, (g + 1) * LRU_BLOCK)
        hf_ref[:, sl] = h_ref[0, pl.ds(g, ts, stride=N_LRU_BLOCKS), :]
        hr_ref[:, sl] = h_ref[1, pl.ds(g, ts, stride=N_LRU_BLOCKS), :]


def _rglru(xr, conv_w, conv_b, w_cat, ba, bx, lam):
    b, s, _ = xr.shape
    ts = TS_LRU
    n_tiles = s // ts
    per = ts // SUBLANES
    n8 = s // SUBLANES
    halo = (None, SUBLANES, LRU_WIDTH)
    tile = (None, ts, LRU_WIDTH)
    rev = lambda i: n_tiles - 1 - i
    const2 = lambda bb, i: (0, 0)
    in_specs = [
        pl.BlockSpec(halo, lambda bb, i: (bb, jnp.maximum(i * per - 1, 0), 0)),
        pl.BlockSpec(tile, lambda bb, i: (bb, i, 0)),
        pl.BlockSpec(halo, lambda bb, i: (bb, jnp.minimum((i + 1) * per, n8 - 1), 0)),
        pl.BlockSpec(halo, lambda bb, i: (bb, jnp.maximum(rev(i) * per - 1, 0), 0)),
        pl.BlockSpec(tile, lambda bb, i: (bb, rev(i), 0)),
        pl.BlockSpec(halo, lambda bb, i: (bb, jnp.minimum((rev(i) + 1) * per, n8 - 1), 0)),
        pl.BlockSpec((CONV_WIDTH, LRU_WIDTH), const2),
        pl.BlockSpec((1, LRU_WIDTH), const2),
        pl.BlockSpec((2, N_LRU_BLOCKS, LRU_BLOCK, 2 * LRU_BLOCK), lambda bb, i: (0, 0, 0, 0)),
        pl.BlockSpec((2, LRU_WIDTH), const2),
        pl.BlockSpec((2, LRU_WIDTH), const2),
        pl.BlockSpec((2, LRU_WIDTH), const2),
    ]
    out = jax.ShapeDtypeStruct((b, s, LRU_WIDTH), F32)
    return pl.pallas_call(
        functools.partial(_rglru_kernel, n_tiles=n_tiles),
        grid=(b, n_tiles),
        in_specs=in_specs,
        out_specs=[pl.BlockSpec(tile, lambda bb, i: (bb, i, 0)),
                   pl.BlockSpec(tile, lambda bb, i: (bb, rev(i), 0))],
        out_shape=[out, out],
        scratch_shapes=[
            pltpu.VMEM((ts + 2 * SUBLANES, LRU_WIDTH), F32),
            pltpu.VMEM((2, ts * N_LRU_BLOCKS, LANES), F32),
            pltpu.VMEM((2, ts * N_LRU_BLOCKS, LANES), F32),
            pltpu.VMEM((2, ts * N_LRU_BLOCKS, LANES), F32),
            pltpu.VMEM((2, SUBLANES, LANES), F32),
        ],
        compiler_params=_cparams(("arbitrary", "arbitrary")),
        name="rglru",
    )(xr, xr, xr, xr, xr, xr, conv_w, conv_b, w_cat, ba, bx, lam)


def _out_proj_kernel(attn_ref, hf_ref, hr_ref, gate_ref, x_ref, ga_ref, gl_ref, w_ref, g2_ref, wq_ref,
                     x1_ref, xn_ref, pq_ref):
    lru = (hf_ref[...] + hr_ref[...]) * _gelu(gate_ref[...])
    na = _rms(attn_ref[...], ga_ref[...]).astype(BF16)
    nl = _rms(lru, gl_ref[...]).astype(BF16)
    mix = jnp.dot(jnp.concatenate([na, nl], axis=-1), w_ref[...], preferred_element_type=F32)
    x1 = x_ref[...] + mix
    x1_ref[...] = x1
    xn = _rms(x1, g2_ref[...]).astype(BF16)
    xn_ref[...] = xn
    pq_ref[...] = jnp.dot(xn, wq_ref[...], preferred_element_type=F32).astype(BF16)


def _out_proj(attn, hf, hr, gate, x, ga, gl, w_out, g2, wq):
    t = x.shape[0]
    tm = TM_OUT
    tok = lambda i: (i, 0)
    const = lambda i: (0, 0)
    half = pl.BlockSpec((tm, ATTN_WIDTH), tok)
    full = pl.BlockSpec((tm, D_MODEL), tok)
    weight = pl.BlockSpec((D_MODEL, D_MODEL), const, pipeline_mode=pl.Buffered(1))
    act = jax.ShapeDtypeStruct((t, D_MODEL), BF16)
    return pl.pallas_call(
        _out_proj_kernel,
        grid=(t // tm,),
        in_specs=[half, half, half, half, full,
                  pl.BlockSpec((1, ATTN_WIDTH), const), pl.BlockSpec((1, LRU_WIDTH), const),
                  weight, pl.BlockSpec((1, D_MODEL), const), weight],
        out_specs=[full, full, full],
        out_shape=[jax.ShapeDtypeStruct((t, D_MODEL), F32), act, act],
        compiler_params=_cparams(("arbitrary",)),
        name="out_proj",
    )(attn, hf, hr, gate, x, ga, gl, w_out, g2, wq)


def _top16_rows(x, codes):
    big = jnp.iinfo(jnp.int32).max
    vals, ids = [], []
    for _ in range(PEER_TOPK):
        m = jnp.max(x, axis=0, keepdims=True)
        idx = jnp.min(jnp.where(x == m, codes, big), axis=0, keepdims=True)
        x = jnp.where(codes == idx, -jnp.inf, x)
        vals.append(m)
        ids.append(idx)
    return jnp.concatenate(vals, axis=0), jnp.concatenate(ids, axis=0)


def _sort16_network():
    def merge(lo, hi, r):
        step = r * 2
        if step < hi - lo:
            yield from merge(lo, hi, step)
            yield from merge(lo + r, hi, step)
            for i in range(lo + r, hi - r, step):
                yield (i, i + r)
        else:
            yield (lo, lo + r)

    def sort(lo, hi):
        if hi - lo >= 1:
            mid = lo + (hi - lo) // 2
            yield from sort(lo, mid)
            yield from sort(mid + 1, hi)
            yield from merge(lo, hi, 1)

    return tuple(sort(0, PEER_TOPK - 1))


SORT16 = _sort16_network()


def _top16_of_keys(sc):
    lanes = sc.shape[1]
    sub = lax.broadcasted_iota(I32, (SUBLANES, lanes), 0)
    v = [sc[SUBLANES * j:SUBLANES * (j + 1), :] for j in range(PEER_TOPK)]
    ix = [sub + SUBLANES * j for j in range(PEER_TOPK)]
    for i, j in SORT16:
        keep = v[i] >= v[j]
        v[i], v[j] = jnp.maximum(v[i], v[j]), jnp.minimum(v[i], v[j])
        ix[i], ix[j] = jnp.where(keep, ix[i], ix[j]), jnp.where(keep, ix[j], ix[i])
    big = jnp.iinfo(jnp.int32).max
    vals, ids = [], []
    for a in range(PEER_TOPK):
        m = jnp.max(v[0], axis=0, keepdims=True)
        idx = jnp.min(jnp.where(v[0] == m, ix[0], big), axis=0, keepdims=True)
        vals.append(m)
        ids.append(idx)
        win = ix[0] == idx
        for j in range(PEER_TOPK - 1 - a):
            v[j] = jnp.where(win, v[j + 1], v[j])
            ix[j] = jnp.where(win, ix[j + 1], ix[j])
    return jnp.concatenate(vals, axis=0), jnp.concatenate(ids, axis=0)


def _route_tile(s1, s2, cand_codes):
    v1, id1 = _top16_of_keys(s1)
    v2, id2 = _top16_of_keys(s2)
    cand = jnp.concatenate(
        [v1[0:1] + v2] + [v1[a:a + 1] + v2[0:8] for a in range(1, 8)] + [v1[8:16] + v2[0:1]], axis=0)
    top, code = _top16_rows(cand, cand_codes)
    ra = code >> 4
    rb = code & (PEER_TOPK - 1)
    i1 = jnp.zeros(ra.shape, I32)
    i2 = jnp.zeros(ra.shape, I32)
    for a in range(PEER_TOPK):
        i1 = jnp.where(ra == a, id1[a:a + 1], i1)
        i2 = jnp.where(rb == a, id2[a:a + 1], i2)
    ex = jnp.exp(top - top[0:1])
    return i1, i2, ex / jnp.sum(ex, axis=0, keepdims=True)


def _peer_kernel(xn_ref, x1_ref, pq_ref, keys_ref, ut_ref, v_ref, y_ref,
                 coef_ref, rt1_ref, rt2_ref, rtg_ref, tk1_ref, tk2_ref, tkg_ref):
    i = pl.program_id(0)
    e = pl.program_id(1)
    tm = xn_ref.shape[0]
    n_q = tm // LANES
    n_blocks = N_EXPERTS // EB_PEER
    i1_per_block = EB_PEER // N_KEYS
    tiles_per_step = pq_ref.shape[0] // LANES
    steps_per_head = n_q // tiles_per_step
    high_half = -(1 << 16)

    def route_part():
        h = e // steps_per_head
        q0 = (e % steps_per_head) * tiles_per_step
        row16 = lax.broadcasted_iota(I32, (PEER_TOPK, LANES), 0)
        row8 = lax.broadcasted_iota(I32, (SUBLANES, LANES), 0)
        cand_codes = jnp.concatenate(
            [row16] + [a * PEER_TOPK + row8 for a in range(1, 8)] + [(row8 + 8) * PEER_TOPK], axis=0)
        nt = (((1,), (1,)), ((), ()))
        rows = pl.ds(pl.multiple_of(h * PEER_TOPK, PEER_TOPK), PEER_TOPK)
        for k in range(tiles_per_step):
            pq = pq_ref[k * LANES:(k + 1) * LANES, :]
            s1 = lax.dot_general(keys_ref[2 * h], pq[:, :PEER_HALF], nt, preferred_element_type=F32)
            s2 = lax.dot_general(keys_ref[2 * h + 1], pq[:, PEER_HALF:], nt, preferred_element_type=F32)
            i1, i2, g = _route_tile(s1, s2, cand_codes)
            rt1_ref[q0 + k, rows, :] = i1
            rt2_ref[q0 + k, rows, :] = i2
            rtg_ref[q0 + k, rows, :] = g

    def dense_part():
        local = (e % (n_blocks // 2)) * i1_per_block
        shift = jnp.where(e < n_blocks // 2, 16, 0)
        i1_per_pass = i1_per_block // DENSE_PASSES
        width = i1_per_pass * N_KEYS
        for p in range(DENSE_PASSES):
            cols = slice(p * width, (p + 1) * width)
            act = _gelu(jnp.dot(xn_ref[...], ut_ref[:, cols], preferred_element_type=F32))
            parts = []
            for jj in range(i1_per_pass):
                word = coef_ref[pl.ds(local + p * i1_per_pass + jj, tm, stride=COEF_PITCH), :]
                cf = lax.bitcast_convert_type(jnp.left_shift(word, shift) & high_half, F32)
                parts.append((cf * act[:, jj * N_KEYS:(jj + 1) * N_KEYS]).astype(BF16))
            y_ref[...] += jnp.dot(jnp.concatenate(parts, axis=-1), v_ref[cols, :], preferred_element_type=F32)

    @pl.when(i == 0)
    def _():
        route_part()

    @pl.when(i > 0)
    def _():
        @pl.when(e == 0)
        def _():
            y_ref[...] = x1_ref[...]
            for q in range(n_q):
                tok = slice(q * LANES, (q + 1) * LANES)
                tk1_ref[tok, :] = rt1_ref[q].T
                tk2_ref[tok, :] = rt2_ref[q].T
                tkg_ref[tok, :] = rtg_ref[q].T

            rows = lax.broadcasted_iota(I32, (N_KEYS, N_TERMS), 0)

            def token_group(gi, carry):
                t0 = pl.multiple_of(gi * COEF_GROUP, COEF_GROUP)
                i1b = tk1_ref[pl.ds(t0, COEF_GROUP), :]
                i2b = tk2_ref[pl.ds(t0, COEF_GROUP), :]
                gb = tkg_ref[pl.ds(t0, COEF_GROUP), :]
                zero = jnp.zeros((N_KEYS, N_TERMS), BF16)
                for uu in range(0, COEF_GROUP, 2):
                    a1, a2 = [], []
                    for v in (uu, uu + 1):
                        a1.append(jnp.where(rows == i1b[v:v + 1, :], gb[v:v + 1, :], 0.0).astype(BF16))
                        a2.append(jnp.where(rows == i2b[v:v + 1, :], 1.0, 0.0).astype(BF16))
                    lhs = jnp.concatenate([jnp.concatenate([a1[0], zero], axis=1),
                                           jnp.concatenate([zero, a1[1]], axis=1)], axis=0)
                    rhs = jnp.concatenate(a2, axis=1)
                    c = lax.dot_general(lhs, rhs, (((1,), (1,)), ((), ())), preferred_element_type=F32)
                    for k, v in enumerate((uu, uu + 1)):
                        lo = c[k * N_KEYS:k * N_KEYS + COEF_ROWS].astype(BF16).astype(F32)
                        hi = c[k * N_KEYS + COEF_ROWS:(k + 1) * N_KEYS].astype(BF16).astype(F32)
                        word = (lax.bitcast_convert_type(hi, I32)
                                | lax.shift_right_logical(lax.bitcast_convert_type(lo, I32), 16))
                        row = pl.multiple_of((t0 + v) * COEF_PITCH, SUBLANES)
                        coef_ref[pl.ds(row, COEF_ROWS), :] = word
                return carry

            lax.fori_loop(0, tm // COEF_GROUP, token_group, 0)

        route_part()
        dense_part()


def _peer(xn, x1, pq, keys, ut, v):
    t = xn.shape[0]
    tm = TM_PEER
    n_tiles = t // tm
    n_q = tm // LANES
    n_blocks = N_EXPERTS // EB_PEER
    steps_per_head = n_blocks // PEER_HEADS
    tiles_per_step = n_q // steps_per_head
    assert steps_per_head * PEER_HEADS == n_blocks and tiles_per_step * steps_per_head == n_q
    prev = lambda i, e: (jnp.maximum(i - 1, 0), 0)
    blk = lambda i, e: jnp.where(i == 0, 0, e)
    once = dict(pipeline_mode=pl.Buffered(1))
    return pl.pallas_call(
        _peer_kernel,
        grid=(n_tiles + 1, n_blocks),
        in_specs=[pl.BlockSpec((tm, D_MODEL), prev, **once),
                  pl.BlockSpec((tm, D_MODEL), prev, **once),
                  pl.BlockSpec((tiles_per_step * LANES, 2 * PEER_HALF),
                               lambda i, e: (jnp.minimum(i, n_tiles - 1) * steps_per_head + e % steps_per_head,
                                             e // steps_per_head)),
                  pl.BlockSpec((2 * PEER_HEADS, N_KEYS, PEER_HALF), lambda i, e: (0, 0, 0)),
                  pl.BlockSpec((D_MODEL, EB_PEER), lambda i, e: (0, blk(i, e))),
                  pl.BlockSpec((EB_PEER, D_MODEL), lambda i, e: (blk(i, e), 0))],
        out_specs=pl.BlockSpec((tm, D_MODEL), prev),
        out_shape=jax.ShapeDtypeStruct((t, D_MODEL), F32),
        scratch_shapes=[pltpu.VMEM((tm * COEF_PITCH, LANES), I32),
                        pltpu.VMEM((n_q, N_TERMS, LANES), I32),
                        pltpu.VMEM((n_q, N_TERMS, LANES), I32),
                        pltpu.VMEM((n_q, N_TERMS, LANES), F32),
                        pltpu.VMEM((tm, N_TERMS), I32),
                        pltpu.VMEM((tm, N_TERMS), I32),
                        pltpu.VMEM((tm, N_TERMS), F32)],
        compiler_params=_cparams(("arbitrary", "arbitrary")),
        name="peer",
    )(xn, x1, pq, keys, ut, v)


def _rope_tables(seq):
    inv = ROPE_THETA ** (-jnp.arange(0, HEAD_DIM, 2, dtype=F32) / HEAD_DIM)
    ang = jnp.arange(seq, dtype=F32)[:, None] * inv[None, :]
    cos, sin = jnp.cos(ang), jnp.sin(ang)
    return jnp.concatenate([cos, cos], axis=-1), jnp.concatenate([-sin, sin], axis=-1)


def _prep_layer(p):
    (ln1_g, w_in, qg, kg, conv_w, conv_b, wa, ba, wx, bx, lam, ga, gl, w_out, g2, wq, keys, u, v) = p
    w_cat = jnp.concatenate([wa, wx], axis=-1).astype(BF16)
    return dict(
        ln1_g=ln1_g.reshape(1, -1), w_in=w_in.astype(BF16), qg=qg.reshape(1, -1), kg=kg.reshape(1, -1),
        conv_w=conv_w, conv_b=conv_b.reshape(1, -1), w_cat=w_cat, ba=ba, bx=bx, lam=lam,
        ga=ga.reshape(1, -1), gl=gl.reshape(1, -1), w_out=w_out.astype(BF16), g2=g2.reshape(1, -1),
        wq=wq.astype(BF16), keys=keys.reshape(2 * PEER_HEADS, N_KEYS, PEER_HALF).astype(BF16),
        ut=u.astype(BF16).T, v=v.astype(BF16))


def _layer(x, w, cos, sin):
    b, s, _ = x.shape
    xt = x.reshape(b * s, D_MODEL)
    q, k, v, xr, gate = _in_proj(xt, w["ln1_g"], w["w_in"], w["qg"], w["kg"], cos, sin, s)
    shp = (b, s, ATTN_WIDTH)
    attn = _dil_attn(q.reshape(shp), k.reshape(shp), v.reshape(shp))
    hf, hr = _rglru(xr.reshape(shp), w["conv_w"], w["conv_b"], w["w_cat"], w["ba"], w["bx"], w["lam"])
    x1, xn, pq = _out_proj(attn.reshape(b * s, -1), hf.reshape(b * s, -1), hr.reshape(b * s, -1), gate, xt,
                           w["ga"], w["gl"], w["w_out"], w["g2"], w["wq"])
    y = _peer(xn, x1, pq, w["keys"], w["ut"], w["v"])
    return y.reshape(b, s, D_MODEL)


def kernel(x_prompt, x_sample, ln1_g, w_in, q_norm_g, k_norm_g, conv_w, conv_b, lru_wa, lru_ba, lru_wx, lru_bx, lru_lam, attn_out_g, lru_out_g, w_out, ln2_g, peer_wq, peer_keys, peer_u, peer_v):
    params = (ln1_g, w_in, q_norm_g, k_norm_g, conv_w, conv_b, lru_wa, lru_ba, lru_wx, lru_bx,
              lru_lam, attn_out_g, lru_out_g, w_out, ln2_g, peer_wq, peer_keys, peer_u, peer_v)
    depth = ln1_g.shape[0]
    layers = [_prep_layer([p[l] for p in params]) for l in range(depth)]

    def trunk(x):
        cos, sin = _rope_tables(x.shape[1])
        for w in layers:
            x = _layer(x, w, cos, sin)
        return x

    return (trunk(x_prompt), trunk(x_sample))
```

```python
import functools
import math

import jax
import jax.numpy as jnp
from jax import lax
from jax.experimental import pallas as pl
from jax.experimental.pallas import tpu as pltpu

F32 = jnp.float32
BF16 = jnp.bfloat16
I32 = jnp.int32

D_MODEL = 2048
HEAD_DIM = 128
N_HEADS = 8
ATTN_WIDTH = N_HEADS * HEAD_DIM
LRU_WIDTH = D_MODEL - ATTN_WIDTH
N_LRU_BLOCKS = 8
LRU_BLOCK = LRU_WIDTH // N_LRU_BLOCKS
CONV_WIDTH = 4
CONV_LEFT = 2
LRU_C = 8.0
DILATIONS = (1, 4, 16)
HALF_WINDOW = 64
ROPE_THETA = 10000.0
N_KEYS = 128
N_EXPERTS = N_KEYS * N_KEYS
PEER_HEADS = 8
PEER_TOPK = 16
PEER_HALF = 128
N_TERMS = PEER_HEADS * PEER_TOPK
EPS = 1e-6
NEG = -1e30

LANES = 128
SUBLANES = 8
VMEM_LIMIT_BYTES = 56 * 1024 * 1024

TM_IN = 256
N_IN_BLOCKS = 5
TP_ATTN = 2048
TQ = 128
ATTN_CLASSES = 4
TS_LRU = 512
TM_OUT = 256
TM_PEER = 512
EB_PEER = 512
DENSE_PASSES = 1
COEF_ROWS = N_KEYS // 2
COEF_PITCH = COEF_ROWS + SUBLANES
COEF_GROUP = 32


def _cparams(sem):
    return pltpu.CompilerParams(dimension_semantics=sem, vmem_limit_bytes=VMEM_LIMIT_BYTES)


def _rms(x, g):
    return x * lax.rsqrt(jnp.mean(x * x, axis=-1, keepdims=True) + EPS) * g


def _gelu(x):
    return 0.5 * x * (1.0 + lax.erf(x * (1.0 / math.sqrt(2.0))))


def _sigmoid(x):
    return 0.5 * jnp.tanh(0.5 * x) + 0.5


def _in_proj_kernel(x_ref, g_ref, w_ref, qg_ref, kg_ref, cos_ref, sin_ref,
                    q_ref, k_ref, v_ref, xr_ref, gate_ref):
    hn = _rms(x_ref[...], g_ref[...]).astype(BF16)
    cos = cos_ref[...]
    sin = sin_ref[...]

    def block(j):
        return jnp.dot(hn, w_ref[:, j * ATTN_WIDTH:(j + 1) * ATTN_WIDTH], preferred_element_type=F32)

    def head_norm_rope(y, g):
        outs = []
        for h in range(N_HEADS):
            t = _rms(y[:, h * HEAD_DIM:(h + 1) * HEAD_DIM], g)
            outs.append(t * cos + pltpu.roll(t, HEAD_DIM // 2, 1) * sin)
        return jnp.concatenate(outs, axis=-1)

    q_ref[...] = head_norm_rope(block(0), qg_ref[...])
    k_ref[...] = head_norm_rope(block(1), kg_ref[...])
    v_ref[...] = block(2)
    xr_ref[...] = block(3)
    gate_ref[...] = block(4)


def _in_proj(x, ln1_g, w_in, qg, kg, cos, sin, seq):
    t = x.shape[0]
    tm = min(TM_IN, seq)
    nblk = seq // tm
    col = ATTN_WIDTH
    tok = lambda i: (i, 0)
    const = lambda i: (0, 0)
    pos = lambda i: (i % nblk, 0)
    out = jax.ShapeDtypeStruct((t, col), F32)
    return pl.pallas_call(
        _in_proj_kernel,
        grid=(t // tm,),
        in_specs=[
            pl.BlockSpec((tm, D_MODEL), tok),
            pl.BlockSpec((1, D_MODEL), const),
            pl.BlockSpec((D_MODEL, N_IN_BLOCKS * col), const, pipeline_mode=pl.Buffered(1)),
            pl.BlockSpec((1, HEAD_DIM), const),
            pl.BlockSpec((1, HEAD_DIM), const),
            pl.BlockSpec((tm, HEAD_DIM), pos),
            pl.BlockSpec((tm, HEAD_DIM), pos),
        ],
        out_specs=[pl.BlockSpec((tm, col), tok)] * 5,
        out_shape=[out] * 5,
        compiler_params=_cparams(("arbitrary",)),
        name="in_proj",
    )(x, ln1_g, w_in, qg, kg, cos, sin)


def _attn_kernel(q_ref, kp_ref, kc_ref, kn_ref, vp_ref, vc_ref, vn_ref, o_ref,
                 qs_ref, ks_ref, vs_ref, acc_ref, m_ref, l_ref, *, n_tiles):
    i = pl.program_id(2)
    tp = q_ref.shape[0]
    nc = ATTN_CLASSES
    halo = HALF_WINDOW * DILATIONS[-1]
    qc = tp // nc
    kc = (tp + 2 * halo) // nc
    hc = halo // nc
    scale = HEAD_DIM ** -0.5
    first = i == 0
    last = i == n_tiles - 1

    for c in range(nc):
        qs_ref[c * qc:(c + 1) * qc, :] = q_ref[pl.ds(c, qc, stride=nc), :]
        for prv, cur, nxt, dst in ((kp_ref, kc_ref, kn_ref, ks_ref), (vp_ref, vc_ref, vn_ref, vs_ref)):
            dst[c * kc:c * kc + hc, :] = prv[pl.ds(tp - halo + c, hc, stride=nc), :]
            dst[c * kc + hc:c * kc + hc + qc, :] = cur[pl.ds(c, qc, stride=nc), :]
            dst[c * kc + hc + qc:(c + 1) * kc, :] = nxt[pl.ds(c, hc, stride=nc), :]

    row = lax.broadcasted_iota(I32, (TQ, 2 * TQ), 0)
    col = lax.broadcasted_iota(I32, (TQ, 2 * TQ), 1)
    band = (col >= row) & (col - row <= 2 * HALF_WINDOW)
    head_ok = (col >= HALF_WINDOW) | jnp.logical_not(first)
    tail_ok = (col < 2 * TQ - HALF_WINDOW) | jnp.logical_not(last)
    per_q = TQ // nc
    per_k = 2 * TQ // nc
    qpos = nc * (row & (per_q - 1)) + (row >> (per_q.bit_length() - 1))
    kpos = nc * (col & (per_k - 1)) + (col >> (per_k.bit_length() - 1))
    band1 = (kpos >= qpos) & (kpos - qpos <= 2 * HALF_WINDOW)
    head_ok1 = (kpos >= HALF_WINDOW) | jnp.logical_not(first)
    tail_ok1 = (kpos < 2 * TQ - HALF_WINDOW) | jnp.logical_not(last)

    def attend(q, k, v, ok):
        s = lax.dot_general(q.astype(BF16), k.astype(BF16), (((1,), (1,)), ((), ())),
                            preferred_element_type=F32) * scale
        s = jnp.where(ok, s, NEG)
        m = jnp.max(s, axis=-1, keepdims=True)
        p = jnp.exp(s - m)
        l = jnp.sum(p, axis=-1, keepdims=True)
        o = jnp.dot(p.astype(BF16), v.astype(BF16), preferred_element_type=F32)
        return o, jnp.broadcast_to(m, (TQ, LANES)), jnp.broadcast_to(l, (TQ, LANES))

    for u in range(tp // TQ):
        qrows = [pl.ds(c * qc + per_q * u, per_q) for c in range(nc)]
        krows = [pl.ds(c * kc + hc - HALF_WINDOW // nc + per_q * u, per_k) for c in range(nc)]
        ok = band1
        if u == 0:
            ok = ok & head_ok1
        if u == tp // TQ - 1:
            ok = ok & tail_ok1
        o, m, l = attend(jnp.concatenate([qs_ref[r, :] for r in qrows], axis=0),
                         jnp.concatenate([ks_ref[r, :] for r in krows], axis=0),
                         jnp.concatenate([vs_ref[r, :] for r in krows], axis=0), ok)
        for c, r in enumerate(qrows):
            piece = slice(c * per_q, (c + 1) * per_q)
            acc_ref[0, r, :] = o[piece]
            m_ref[0, r, :] = m[piece]
            l_ref[0, r, :] = l[piece]

    for c in range(nc):
        for u in range(qc // TQ):
            qrow = pl.ds(c * qc + u * TQ, TQ)
            krow = pl.ds(c * kc + hc - HALF_WINDOW + u * TQ, 2 * TQ)
            ok = band
            if u == 0:
                ok = ok & head_ok
            if u == qc // TQ - 1:
                ok = ok & tail_ok
            o, m, l = attend(qs_ref[qrow, :], ks_ref[krow, :], vs_ref[krow, :], ok)
            acc_ref[1, qrow, :] = o
            m_ref[1, qrow, :] = m
            l_ref[1, qrow, :] = l
        for r in range(nc):
            qrow = pl.ds(c * qc + r, TQ, stride=nc)
            krow = pl.ds(c * kc + r, 2 * TQ, stride=nc)
            o, m, l = attend(qs_ref[qrow, :], ks_ref[krow, :], vs_ref[krow, :], band & head_ok & tail_ok)
            acc_ref[2, qrow, :] = o
            m_ref[2, qrow, :] = m
            l_ref[2, qrow, :] = l

    m_all = jnp.maximum(jnp.maximum(m_ref[0], m_ref[1]), m_ref[2])
    num = jnp.zeros((tp, LANES), F32)
    den = jnp.zeros((tp, LANES), F32)
    for g in range(len(DILATIONS)):
        w = jnp.exp(m_ref[g] - m_all)
        num = num + w * acc_ref[g]
        den = den + w * l_ref[g]
    res = num / den
    for c in range(nc):
        o_ref[pl.ds(c, qc, stride=nc), :] = res[c * qc:(c + 1) * qc]


def _dil_attn(q, k, v):
    b, s, _ = q.shape
    tp = TP_ATTN
    n_tiles = s // tp
    cur = lambda bb, h, i: (bb, i, h)
    prv = lambda bb, h, i: (bb, jnp.maximum(i - 1, 0), h)
    nxt = lambda bb, h, i: (bb, jnp.minimum(i + 1, n_tiles - 1), h)
    blk = (None, tp, HEAD_DIM)
    return pl.pallas_call(
        functools.partial(_attn_kernel, n_tiles=n_tiles),
        grid=(b, N_HEADS, n_tiles),
        in_specs=[
            pl.BlockSpec(blk, cur),
            pl.BlockSpec(blk, prv), pl.BlockSpec(blk, cur), pl.BlockSpec(blk, nxt),
            pl.BlockSpec(blk, prv), pl.BlockSpec(blk, cur), pl.BlockSpec(blk, nxt),
        ],
        out_specs=pl.BlockSpec(blk, cur),
        out_shape=jax.ShapeDtypeStruct((b, s, ATTN_WIDTH), F32),
        scratch_shapes=[pltpu.VMEM((tp, LANES), F32),
                        pltpu.VMEM((tp + 2 * HALF_WINDOW * DILATIONS[-1], LANES), F32),
                        pltpu.VMEM((tp + 2 * HALF_WINDOW * DILATIONS[-1], LANES), F32)]
        + [pltpu.VMEM((len(DILATIONS), tp, LANES), F32)] * 3,
        compiler_params=_cparams(("arbitrary", "arbitrary", "arbitrary")),
        name="dil_attn",
    )(q, k, k, k, v, v, v)


def _rglru_kernel(xf_p, xf_c, xf_n, xr_p, xr_c, xr_n, cw_ref, cb_ref, w_ref, ba_ref, bx_ref,
                  lam_ref, hf_ref, hr_ref, xe_ref, xc_s, a_ref, b_ref, h_ref, carry_ref, *, n_tiles):
    i = pl.program_id(1)
    ts = xf_c.shape[0]

    @pl.when(i == 0)
    def _():
        carry_ref[...] = jnp.zeros_like(carry_ref)

    def gates(direction, xp, xc_ref, xn, at_start, at_end):
        nb = N_LRU_BLOCKS
        for g in range(nb):
            sl = slice(g * LRU_BLOCK, (g + 1) * LRU_BLOCK)
            xe_ref[pl.ds(g, SUBLANES, stride=nb), :] = jnp.where(at_start, 0.0, xp[:, sl])
            xe_ref[pl.ds(SUBLANES * nb + g, ts, stride=nb), :] = xc_ref[:, sl]
            xe_ref[pl.ds((SUBLANES + ts) * nb + g, SUBLANES, stride=nb), :] = jnp.where(at_end, 0.0, xn[:, sl])
        base = SUBLANES - CONV_LEFT
        acc = cb_ref[...][None]
        for jj in range(CONV_WIDTH):
            tap = xe_ref[pl.ds((base + jj) * nb, ts * nb), :].reshape(ts, nb, LANES)
            acc = acc + tap * cw_ref[jj][None]
        xc_s[...] = acc.reshape(ts * nb, LANES)
        lam = lam_ref[direction:direction + 1, :]
        z = -lam
        softplus = jnp.maximum(z, 0.0) + jnp.log(1.0 + jnp.exp(-jnp.abs(z)))
        c = -LRU_C * softplus
        for g in range(N_LRU_BLOCKS):
            sl = slice(g * LRU_BLOCK, (g + 1) * LRU_BLOCK)
            xg = xc_s[pl.ds(g, ts, stride=nb), :]
            y = jnp.dot(xg.astype(BF16), w_ref[direction, g], preferred_element_type=F32)
            r = _sigmoid(y[:, :LRU_BLOCK] + ba_ref[direction:direction + 1, sl])
            gi = _sigmoid(y[:, LRU_BLOCK:] + bx_ref[direction:direction + 1, sl])
            log_a = c[:, sl] * r
            a = jnp.exp(log_a)
            bt = jnp.sqrt(-jnp.tanh(log_a) * (a * a + 1.0)) * gi * xg
            a_ref[direction, pl.ds(g, ts, stride=N_LRU_BLOCKS), :] = a
            b_ref[direction, pl.ds(g, ts, stride=N_LRU_BLOCKS), :] = bt

    gates(0, xf_p, xf_c, xf_n, i == 0, i == n_tiles - 1)
    gates(1, xr_p, xr_c, xr_n, i == n_tiles - 1, i == 0)

    unroll = 8

    def step(it, carry):
        hf, hr = carry
        for uu in range(unroll):
            tf = pl.multiple_of((it * unroll + uu) * SUBLANES, SUBLANES)
            tr = pl.multiple_of((ts - 1 - it * unroll - uu) * SUBLANES, SUBLANES)
            hf = a_ref[0, pl.ds(tf, SUBLANES), :] * hf + b_ref[0, pl.ds(tf, SUBLANES), :]
            hr = a_ref[1, pl.ds(tr, SUBLANES), :] * hr + b_ref[1, pl.ds(tr, SUBLANES), :]
            h_ref[0, pl.ds(tf, SUBLANES), :] = hf
            h_ref[1, pl.ds(tr, SUBLANES), :] = hr
        return hf, hr

    hf, hr = lax.fori_loop(0, ts // unroll, step, (carry_ref[0], carry_ref[1]))
    carry_ref[0] = hf
    carry_ref[1] = hr
    for g in range(N_LRU_BLOCKS):
        sl = slice(g * LRU_BLOCK, (g + 1) * LRU_BLOCK)
        hf_ref[:, sl] = h_ref[0, pl.ds(g, ts, stride=N_LRU_BLOCKS), :]
        hr_ref[:, sl] = h_ref[1, pl.ds(g, ts, stride=N_LRU_BLOCKS), :]


def _rglru(xr, conv_w, conv_b, w_cat, ba, bx, lam):
    b, s, _ = xr.shape
    ts = TS_LRU
    n_tiles = s // ts
    per = ts // SUBLANES
    n8 = s // SUBLANES
    halo = (None, SUBLANES, LRU_WIDTH)
    tile = (None, ts, LRU_WIDTH)
    rev = lambda i: n_tiles - 1 - i
    const2 = lambda bb, i: (0, 0)
    in_specs = [
        pl.BlockSpec(halo, lambda bb, i: (bb, jnp.maximum(i * per - 1, 0), 0)),
        pl.BlockSpec(tile, lambda bb, i: (bb, i, 0)),
        pl.BlockSpec(halo, lambda bb, i: (bb, jnp.minimum((i + 1) * per, n8 - 1), 0)),
        pl.BlockSpec(halo, lambda bb, i: (bb, jnp.maximum(rev(i) * per - 1, 0), 0)),
        pl.BlockSpec(tile, lambda bb, i: (bb, rev(i), 0)),
        pl.BlockSpec(halo, lambda bb, i: (bb, jnp.minimum((rev(i) + 1) * per, n8 - 1), 0)),
        pl.BlockSpec((CONV_WIDTH, N_LRU_BLOCKS, LRU_BLOCK), lambda bb, i: (0, 0, 0)),
        pl.BlockSpec((N_LRU_BLOCKS, LRU_BLOCK), const2),
        pl.BlockSpec((2, N_LRU_BLOCKS, LRU_BLOCK, 2 * LRU_BLOCK), lambda bb, i: (0, 0, 0, 0)),
        pl.BlockSpec((2, LRU_WIDTH), const2),
        pl.BlockSpec((2, LRU_WIDTH), const2),
        pl.BlockSpec((2, LRU_WIDTH), const2),
    ]
    out = jax.ShapeDtypeStruct((b, s, LRU_WIDTH), F32)
    return pl.pallas_call(
        functools.partial(_rglru_kernel, n_tiles=n_tiles),
        grid=(b, n_tiles),
        in_specs=in_specs,
        out_specs=[pl.BlockSpec(tile, lambda bb, i: (bb, i, 0)),
                   pl.BlockSpec(tile, lambda bb, i: (bb, rev(i), 0))],
        out_shape=[out, out],
        scratch_shapes=[
            pltpu.VMEM(((ts + 2 * SUBLANES) * N_LRU_BLOCKS, LANES), F32),
            pltpu.VMEM((ts * N_LRU_BLOCKS, LANES), F32),
            pltpu.VMEM((2, ts * N_LRU_BLOCKS, LANES), F32),
            pltpu.VMEM((2, ts * N_LRU_BLOCKS, LANES), F32),
            pltpu.VMEM((2, ts * N_LRU_BLOCKS, LANES), F32),
            pltpu.VMEM((2, SUBLANES, LANES), F32),
        ],
        compiler_params=_cparams(("arbitrary", "arbitrary")),
        name="rglru",
    )(xr, xr, xr, xr, xr, xr, conv_w, conv_b, w_cat, ba, bx, lam)


def _out_proj_kernel(attn_ref, hf_ref, hr_ref, gate_ref, x_ref, ga_ref, gl_ref, w_ref, g2_ref, wq_ref,
                     x1_ref, xn_ref, pq_ref):
    lru = (hf_ref[...] + hr_ref[...]) * _gelu(gate_ref[...])
    na = _rms(attn_ref[...], ga_ref[...]).astype(BF16)
    nl = _rms(lru, gl_ref[...]).astype(BF16)
    mix = jnp.dot(jnp.concatenate([na, nl], axis=-1), w_ref[...], preferred_element_type=F32)
    x1 = x_ref[...] + mix
    x1_ref[...] = x1
    xn = _rms(x1, g2_ref[...]).astype(BF16)
    xn_ref[...] = xn
    pq_ref[...] = jnp.dot(xn, wq_ref[...], preferred_element_type=F32).astype(BF16)


def _out_proj(attn, hf, hr, gate, x, ga, gl, w_out, g2, wq):
    t = x.shape[0]
    tm = TM_OUT
    tok = lambda i: (i, 0)
    const = lambda i: (0, 0)
    half = pl.BlockSpec((tm, ATTN_WIDTH), tok)
    full = pl.BlockSpec((tm, D_MODEL), tok)
    weight = pl.BlockSpec((D_MODEL, D_MODEL), const, pipeline_mode=pl.Buffered(1))
    act = jax.ShapeDtypeStruct((t, D_MODEL), BF16)
    return pl.pallas_call(
        _out_proj_kernel,
        grid=(t // tm,),
        in_specs=[half, half, half, half, full,
                  pl.BlockSpec((1, ATTN_WIDTH), const), pl.BlockSpec((1, LRU_WIDTH), const),
                  weight, pl.BlockSpec((1, D_MODEL), const), weight],
        out_specs=[full, full, full],
        out_shape=[jax.ShapeDtypeStruct((t, D_MODEL), F32), act, act],
        compiler_params=_cparams(("arbitrary",)),
        name="out_proj",
    )(attn, hf, hr, gate, x, ga, gl, w_out, g2, wq)


def _top16_rows(x, codes):
    big = jnp.iinfo(jnp.int32).max
    vals, ids = [], []
    for _ in range(PEER_TOPK):
        m = jnp.max(x, axis=0, keepdims=True)
        idx = jnp.min(jnp.where(x == m, codes, big), axis=0, keepdims=True)
        x = jnp.where(codes == idx, -jnp.inf, x)
        vals.append(m)
        ids.append(idx)
    return jnp.concatenate(vals, axis=0), jnp.concatenate(ids, axis=0)


def _sort16_network():
    def merge(lo, hi, r):
        step = r * 2
        if step < hi - lo:
            yield from merge(lo, hi, step)
            yield from merge(lo + r, hi, step)
            for i in range(lo + r, hi - r, step):
                yield (i, i + r)
        else:
            yield (lo, lo + r)

    def sort(lo, hi):
        if hi - lo >= 1:
            mid = lo + (hi - lo) // 2
            yield from sort(lo, mid)
            yield from sort(mid + 1, hi)
            yield from merge(lo, hi, 1)

    return tuple(sort(0, PEER_TOPK - 1))


SORT16 = _sort16_network()


def _top16_of_keys(sc):
    lanes = sc.shape[1]
    sub = lax.broadcasted_iota(I32, (SUBLANES, lanes), 0)
    v = [sc[SUBLANES * j:SUBLANES * (j + 1), :] for j in range(PEER_TOPK)]
    ix = [sub + SUBLANES * j for j in range(PEER_TOPK)]
    for i, j in SORT16:
        keep = v[i] >= v[j]
        v[i], v[j] = jnp.maximum(v[i], v[j]), jnp.minimum(v[i], v[j])
        ix[i], ix[j] = jnp.where(keep, ix[i], ix[j]), jnp.where(keep, ix[j], ix[i])
    big = jnp.iinfo(jnp.int32).max
    vals, ids = [], []
    for a in range(PEER_TOPK):
        m = jnp.max(v[0], axis=0, keepdims=True)
        idx = jnp.min(jnp.where(v[0] == m, ix[0], big), axis=0, keepdims=True)
        vals.append(m)
        ids.append(idx)
        win = ix[0] == idx
        for j in range(PEER_TOPK - 1 - a):
            v[j] = jnp.where(win, v[j + 1], v[j])
            ix[j] = jnp.where(win, ix[j + 1], ix[j])
    return jnp.concatenate(vals, axis=0), jnp.concatenate(ids, axis=0)


def _route_tile(s1, s2, cand_codes):
    v1, id1 = _top16_of_keys(s1)
    v2, id2 = _top16_of_keys(s2)
    cand = jnp.concatenate(
        [v1[0:1] + v2] + [v1[a:a + 1] + v2[0:8] for a in range(1, 8)] + [v1[8:16] + v2[0:1]], axis=0)
    top, code = _top16_rows(cand, cand_codes)
    ra = code >> 4
    rb = code & (PEER_TOPK - 1)
    i1 = jnp.zeros(ra.shape, I32)
    i2 = jnp.zeros(ra.shape, I32)
    for a in range(PEER_TOPK):
        i1 = jnp.where(ra == a, id1[a:a + 1], i1)
        i2 = jnp.where(rb == a, id2[a:a + 1], i2)
    ex = jnp.exp(top - top[0:1])
    return i1, i2, ex / jnp.sum(ex, axis=0, keepdims=True)


def _peer_kernel(xn_ref, x1_ref, pq_ref, keys_ref, ut_ref, v_ref, y_ref,
                 coef_ref, rt1_ref, rt2_ref, rtg_ref, tk1_ref, tk2_ref, tkg_ref):
    i = pl.program_id(0)
    e = pl.program_id(1)
    tm = xn_ref.shape[0]
    n_q = tm // LANES
    n_blocks = N_EXPERTS // EB_PEER
    i1_per_block = EB_PEER // N_KEYS
    tiles_per_step = pq_ref.shape[0] // LANES
    steps_per_head = n_q // tiles_per_step
    high_half = -(1 << 16)

    def route_part():
        h = e // steps_per_head
        q0 = (e % steps_per_head) * tiles_per_step
        row16 = lax.broadcasted_iota(I32, (PEER_TOPK, LANES), 0)
        row8 = lax.broadcasted_iota(I32, (SUBLANES, LANES), 0)
        cand_codes = jnp.concatenate(
            [row16] + [a * PEER_TOPK + row8 for a in range(1, 8)] + [(row8 + 8) * PEER_TOPK], axis=0)
        nt = (((1,), (1,)), ((), ()))
        rows = pl.ds(pl.multiple_of(h * PEER_TOPK, PEER_TOPK), PEER_TOPK)
        for k in range(tiles_per_step):
            pq = pq_ref[k * LANES:(k + 1) * LANES, :]
            s1 = lax.dot_general(keys_ref[2 * h], pq[:, :PEER_HALF], nt, preferred_element_type=F32)
            s2 = lax.dot_general(keys_ref[2 * h + 1], pq[:, PEER_HALF:], nt, preferred_element_type=F32)
            i1, i2, g = _route_tile(s1, s2, cand_codes)
            rt1_ref[q0 + k, rows, :] = i1
            rt2_ref[q0 + k, rows, :] = i2
            rtg_ref[q0 + k, rows, :] = g

    def dense_part():
        local = (e % (n_blocks // 2)) * i1_per_block
        shift = jnp.where(e < n_blocks // 2, 16, 0)
        i1_per_pass = i1_per_block // DENSE_PASSES
        width = i1_per_pass * N_KEYS
        for p in range(DENSE_PASSES):
            cols = slice(p * width, (p + 1) * width)
            act = _gelu(jnp.dot(xn_ref[...], ut_ref[:, cols], preferred_element_type=F32))
            parts = []
            for jj in range(i1_per_pass):
                word = coef_ref[pl.ds(local + p * i1_per_pass + jj, tm, stride=COEF_PITCH), :]
                cf = lax.bitcast_convert_type(jnp.left_shift(word, shift) & high_half, F32)
                parts.append((cf * act[:, jj * N_KEYS:(jj + 1) * N_KEYS]).astype(BF16))
            y_ref[...] += jnp.dot(jnp.concatenate(parts, axis=-1), v_ref[cols, :], preferred_element_type=F32)

    @pl.when(i == 0)
    def _():
        route_part()

    @pl.when(i > 0)
    def _():
        @pl.when(e == 0)
        def _():
            y_ref[...] = x1_ref[...]
            for q in range(n_q):
                tok = slice(q * LANES, (q + 1) * LANES)
                tk1_ref[tok, :] = rt1_ref[q].T
                tk2_ref[tok, :] = rt2_ref[q].T
                tkg_ref[tok, :] = rtg_ref[q].T

            rows = lax.broadcasted_iota(I32, (N_KEYS, N_TERMS), 0)

            def token_group(gi, carry):
                t0 = pl.multiple_of(gi * COEF_GROUP, COEF_GROUP)
                i1b = tk1_ref[pl.ds(t0, COEF_GROUP), :]
                i2b = tk2_ref[pl.ds(t0, COEF_GROUP), :]
                gb = tkg_ref[pl.ds(t0, COEF_GROUP), :]
                zero = jnp.zeros((N_KEYS, N_TERMS), BF16)
                for uu in range(0, COEF_GROUP, 2):
                    a1, a2 = [], []
                    for v in (uu, uu + 1):
                        a1.append(jnp.where(rows == i1b[v:v + 1, :], gb[v:v + 1, :], 0.0).astype(BF16))
                        a2.append(jnp.where(rows == i2b[v:v + 1, :], 1.0, 0.0).astype(BF16))
                    lhs = jnp.concatenate([jnp.concatenate([a1[0], zero], axis=1),
                                           jnp.concatenate([zero, a1[1]], axis=1)], axis=0)
                    rhs = jnp.concatenate(a2, axis=1)
                    c = lax.dot_general(lhs, rhs, (((1,), (1,)), ((), ())), preferred_element_type=F32)
                    for k, v in enumerate((uu, uu + 1)):
                        lo = c[k * N_KEYS:k * N_KEYS + COEF_ROWS].astype(BF16).astype(F32)
                        hi = c[k * N_KEYS + COEF_ROWS:(k + 1) * N_KEYS].astype(BF16).astype(F32)
                        word = (lax.bitcast_convert_type(hi, I32)
                                | lax.shift_right_logical(lax.bitcast_convert_type(lo, I32), 16))
                        row = pl.multiple_of((t0 + v) * COEF_PITCH, SUBLANES)
                        coef_ref[pl.ds(row, COEF_ROWS), :] = word
                return carry

            lax.fori_loop(0, tm // COEF_GROUP, token_group, 0)

        route_part()
        dense_part()


def _peer(xn, x1, pq, keys, ut, v):
    t = xn.shape[0]
    tm = TM_PEER
    n_tiles = t // tm
    n_q = tm // LANES
    n_blocks = N_EXPERTS // EB_PEER
    steps_per_head = n_blocks // PEER_HEADS
    tiles_per_step = n_q // steps_per_head
    assert steps_per_head * PEER_HEADS == n_blocks and tiles_per_step * steps_per_head == n_q
    prev = lambda i, e: (jnp.maximum(i - 1, 0), 0)
    blk = lambda i, e: jnp.where(i == 0, 0, e)
    return pl.pallas_call(
        _peer_kernel,
        grid=(n_tiles + 1, n_blocks),
        in_specs=[pl.BlockSpec((tm, D_MODEL), prev),
                  pl.BlockSpec((tm, D_MODEL), prev),
                  pl.BlockSpec((tiles_per_step * LANES, 2 * PEER_HALF),
                               lambda i, e: (jnp.minimum(i, n_tiles - 1) * steps_per_head + e % steps_per_head,
                                             e // steps_per_head)),
                  pl.BlockSpec((2 * PEER_HEADS, N_KEYS, PEER_HALF), lambda i, e: (0, 0, 0)),
                  pl.BlockSpec((D_MODEL, EB_PEER), lambda i, e: (0, blk(i, e))),
                  pl.BlockSpec((EB_PEER, D_MODEL), lambda i, e: (blk(i, e), 0))],
        out_specs=pl.BlockSpec((tm, D_MODEL), prev),
        out_shape=jax.ShapeDtypeStruct((t, D_MODEL), F32),
        scratch_shapes=[pltpu.VMEM((tm * COEF_PITCH, LANES), I32),
                        pltpu.VMEM((n_q, N_TERMS, LANES), I32),
                        pltpu.VMEM((n_q, N_TERMS, LANES), I32),
                        pltpu.VMEM((n_q, N_TERMS, LANES), F32),
                        pltpu.VMEM((tm, N_TERMS), I32),
                        pltpu.VMEM((tm, N_TERMS), I32),
                        pltpu.VMEM((tm, N_TERMS), F32)],
        compiler_params=_cparams(("arbitrary", "arbitrary")),
        name="peer",
    )(xn, x1, pq, keys, ut, v)


def _rope_tables(seq):
    inv = ROPE_THETA ** (-jnp.arange(0, HEAD_DIM, 2, dtype=F32) / HEAD_DIM)
    ang = jnp.arange(seq, dtype=F32)[:, None] * inv[None, :]
    cos, sin = jnp.cos(ang), jnp.sin(ang)
    return jnp.concatenate([cos, cos], axis=-1), jnp.concatenate([-sin, sin], axis=-1)


def _prep_layer(p):
    (ln1_g, w_in, qg, kg, conv_w, conv_b, wa, ba, wx, bx, lam, ga, gl, w_out, g2, wq, keys, u, v) = p
    w_cat = jnp.concatenate([wa, wx], axis=-1).astype(BF16)
    return dict(
        ln1_g=ln1_g.reshape(1, -1), w_in=w_in.astype(BF16), qg=qg.reshape(1, -1), kg=kg.reshape(1, -1),
        conv_w=conv_w.reshape(CONV_WIDTH, N_LRU_BLOCKS, LRU_BLOCK), conv_b=conv_b.reshape(N_LRU_BLOCKS, LRU_BLOCK),
        w_cat=w_cat, ba=ba, bx=bx, lam=lam,
        ga=ga.reshape(1, -1), gl=gl.reshape(1, -1), w_out=w_out.astype(BF16), g2=g2.reshape(1, -1),
        wq=wq.astype(BF16), keys=keys.reshape(2 * PEER_HEADS, N_KEYS, PEER_HALF).astype(BF16),
        ut=u.astype(BF16).T, v=v.astype(BF16))


def _layer(x, w, cos, sin):
    b, s, _ = x.shape
    xt = x.reshape(b * s, D_MODEL)
    q, k, v, xr, gate = _in_proj(xt, w["ln1_g"], w["w_in"], w["qg"], w["kg"], cos, sin, s)
    shp = (b, s, ATTN_WIDTH)
    attn = _dil_attn(q.reshape(shp), k.reshape(shp), v.reshape(shp))
    hf, hr = _rglru(xr.reshape(shp), w["conv_w"], w["conv_b"], w["w_cat"], w["ba"], w["bx"], w["lam"])
    x1, xn, pq = _out_proj(attn.reshape(b * s, -1), hf.reshape(b * s, -1), hr.reshape(b * s, -1), gate, xt,
                           w["ga"], w["gl"], w["w_out"], w["g2"], w["wq"])
    y = _peer(xn, x1, pq, w["keys"], w["ut"], w["v"])
    return y.reshape(b, s, D_MODEL)


def kernel(x_prompt, x_sample, ln1_g, w_in, q_norm_g, k_norm_g, conv_w, conv_b, lru_wa, lru_ba, lru_wx, lru_bx, lru_lam, attn_out_g, lru_out_g, w_out, ln2_g, peer_wq, peer_keys, peer_u, peer_v):
    params = (ln1_g, w_in, q_norm_g, k_norm_g, conv_w, conv_b, lru_wa, lru_ba, lru_wx, lru_bx,
              lru_lam, attn_out_g, lru_out_g, w_out, ln2_g, peer_wq, peer_keys, peer_u, peer_v)
    depth = ln1_g.shape[0]
    layers = [_prep_layer([p[l] for p in params]) for l in range(depth)]

    def trunk(x):
        cos, sin = _rope_tables(x.shape[1])
        for w in layers:
            x = _layer(x, w, cos, sin)
        return x

    return (trunk(x_prompt), trunk(x_sample))
```

```python
import functools
import math

import jax
import jax.numpy as jnp
from jax import lax
from jax.experimental import pallas as pl
from jax.experimental.pallas import tpu as pltpu

F32 = jnp.float32
BF16 = jnp.bfloat16
I32 = jnp.int32

D_MODEL = 2048
HEAD_DIM = 128
N_HEADS = 8
ATTN_WIDTH = N_HEADS * HEAD_DIM
LRU_WIDTH = D_MODEL - ATTN_WIDTH
N_LRU_BLOCKS = 8
LRU_BLOCK = LRU_WIDTH // N_LRU_BLOCKS
CONV_WIDTH = 4
CONV_LEFT = 2
LRU_C = 8.0
DILATIONS = (1, 4, 16)
HALF_WINDOW = 64
ROPE_THETA = 10000.0
N_KEYS = 128
N_EXPERTS = N_KEYS * N_KEYS
PEER_HEADS = 8
PEER_TOPK = 16
PEER_HALF = 128
N_TERMS = PEER_HEADS * PEER_TOPK
EPS = 1e-6
NEG = -1e30

LANES = 128
SUBLANES = 8
VMEM_LIMIT_BYTES = 56 * 1024 * 1024

TM_IN = 256
N_IN_BLOCKS = 5
TP_ATTN = 2048
TQ = 128
ATTN_CLASSES = 4
TS_LRU = 512
TM_OUT = 256
TM_PEER = 512
EB_PEER = 512
DENSE_PASSES = 1
COEF_ROWS = N_KEYS // 2
COEF_PITCH = COEF_ROWS + SUBLANES
COEF_GROUP = 32


def _cparams(sem):
    return pltpu.CompilerParams(dimension_semantics=sem, vmem_limit_bytes=VMEM_LIMIT_BYTES)


def _rms(x, g):
    return x * lax.rsqrt(jnp.mean(x * x, axis=-1, keepdims=True) + EPS) * g


def _gelu(x):
    return 0.5 * x * (1.0 + lax.erf(x * (1.0 / math.sqrt(2.0))))


def _sigmoid(x):
    return 0.5 * jnp.tanh(0.5 * x) + 0.5


def _in_proj_kernel(x_ref, g_ref, w_ref, qg_ref, kg_ref, cos_ref, sin_ref,
                    q_ref, k_ref, v_ref, xr_ref, gate_ref):
    hn = _rms(x_ref[...], g_ref[...]).astype(BF16)
    cos = cos_ref[...]
    sin = sin_ref[...]

    def block(j):
        return jnp.dot(hn, w_ref[:, j * ATTN_WIDTH:(j + 1) * ATTN_WIDTH], preferred_element_type=F32)

    def head_norm_rope(y, g):
        outs = []
        for h in range(N_HEADS):
            t = _rms(y[:, h * HEAD_DIM:(h + 1) * HEAD_DIM], g)
            outs.append(t * cos + pltpu.roll(t, HEAD_DIM // 2, 1) * sin)
        return jnp.concatenate(outs, axis=-1)

    q_ref[...] = head_norm_rope(block(0), qg_ref[...])
    k_ref[...] = head_norm_rope(block(1), kg_ref[...])
    v_ref[...] = block(2)
    xr_ref[...] = block(3)
    gate_ref[...] = block(4)


def _in_proj(x, ln1_g, w_in, qg, kg, cos, sin, seq):
    t = x.shape[0]
    tm = min(TM_IN, seq)
    nblk = seq // tm
    col = ATTN_WIDTH
    tok = lambda i: (i, 0)
    const = lambda i: (0, 0)
    pos = lambda i: (i % nblk, 0)
    out = jax.ShapeDtypeStruct((t, col), F32)
    return pl.pallas_call(
        _in_proj_kernel,
        grid=(t // tm,),
        in_specs=[
            pl.BlockSpec((tm, D_MODEL), tok),
            pl.BlockSpec((1, D_MODEL), const),
            pl.BlockSpec((D_MODEL, N_IN_BLOCKS * col), const, pipeline_mode=pl.Buffered(1)),
            pl.BlockSpec((1, HEAD_DIM), const),
            pl.BlockSpec((1, HEAD_DIM), const),
            pl.BlockSpec((tm, HEAD_DIM), pos),
            pl.BlockSpec((tm, HEAD_DIM), pos),
        ],
        out_specs=[pl.BlockSpec((tm, col), tok)] * 5,
        out_shape=[out] * 5,
        compiler_params=_cparams(("arbitrary",)),
        name="in_proj",
    )(x, ln1_g, w_in, qg, kg, cos, sin)


def _attn_kernel(q_ref, kp_ref, kc_ref, kn_ref, vp_ref, vc_ref, vn_ref, o_ref,
                 qs_ref, ks_ref, vs_ref, acc_ref, m_ref, l_ref, *, n_tiles):
    i = pl.program_id(2)
    tp = q_ref.shape[0]
    nc = ATTN_CLASSES
    halo = HALF_WINDOW * DILATIONS[-1]
    qc = tp // nc
    kc = (tp + 2 * halo) // nc
    hc = halo // nc
    scale = HEAD_DIM ** -0.5
    first = i == 0
    last = i == n_tiles - 1

    for c in range(nc):
        qs_ref[c * qc:(c + 1) * qc, :] = q_ref[pl.ds(c, qc, stride=nc), :]
        for prv, cur, nxt, dst in ((kp_ref, kc_ref, kn_ref, ks_ref), (vp_ref, vc_ref, vn_ref, vs_ref)):
            dst[c * kc:c * kc + hc, :] = prv[pl.ds(tp - halo + c, hc, stride=nc), :]
            dst[c * kc + hc:c * kc + hc + qc, :] = cur[pl.ds(c, qc, stride=nc), :]
            dst[c * kc + hc + qc:(c + 1) * kc, :] = nxt[pl.ds(c, hc, stride=nc), :]

    row = lax.broadcasted_iota(I32, (TQ, 2 * TQ), 0)
    col = lax.broadcasted_iota(I32, (TQ, 2 * TQ), 1)
    band = (col >= row) & (col - row <= 2 * HALF_WINDOW)
    head_ok = (col >= HALF_WINDOW) | jnp.logical_not(first)
    tail_ok = (col < 2 * TQ - HALF_WINDOW) | jnp.logical_not(last)
    per_q = TQ // nc
    per_k = 2 * TQ // nc
    qpos = nc * (row & (per_q - 1)) + (row >> (per_q.bit_length() - 1))
    kpos = nc * (col & (per_k - 1)) + (col >> (per_k.bit_length() - 1))
    band1 = (kpos >= qpos) & (kpos - qpos <= 2 * HALF_WINDOW)
    head_ok1 = (kpos >= HALF_WINDOW) | jnp.logical_not(first)
    tail_ok1 = (kpos < 2 * TQ - HALF_WINDOW) | jnp.logical_not(last)

    def attend(q, k, v, ok):
        s = lax.dot_general(q.astype(BF16), k.astype(BF16), (((1,), (1,)), ((), ())),
                            preferred_element_type=F32) * scale
        s = jnp.where(ok, s, NEG)
        m = jnp.max(s, axis=-1, keepdims=True)
        p = jnp.exp(s - m)
        l = jnp.sum(p, axis=-1, keepdims=True)
        o = jnp.dot(p.astype(BF16), v.astype(BF16), preferred_element_type=F32)
        return o, jnp.broadcast_to(m, (TQ, LANES)), jnp.broadcast_to(l, (TQ, LANES))

    for u in range(tp // TQ):
        qrows = [pl.ds(c * qc + per_q * u, per_q) for c in range(nc)]
        krows = [pl.ds(c * kc + hc - HALF_WINDOW // nc + per_q * u, per_k) for c in range(nc)]
        ok = band1
        if u == 0:
            ok = ok & head_ok1
        if u == tp // TQ - 1:
            ok = ok & tail_ok1
        o, m, l = attend(jnp.concatenate([qs_ref[r, :] for r in qrows], axis=0),
                         jnp.concatenate([ks_ref[r, :] for r in krows], axis=0),
                         jnp.concatenate([vs_ref[r, :] for r in krows], axis=0), ok)
        for c, r in enumerate(qrows):
            piece = slice(c * per_q, (c + 1) * per_q)
            acc_ref[0, r, :] = o[piece]
            m_ref[0, r, :] = m[piece]
            l_ref[0, r, :] = l[piece]

    for c in range(nc):
        for u in range(qc // TQ):
            qrow = pl.ds(c * qc + u * TQ, TQ)
            krow = pl.ds(c * kc + hc - HALF_WINDOW + u * TQ, 2 * TQ)
            ok = band
            if u == 0:
                ok = ok & head_ok
            if u == qc // TQ - 1:
                ok = ok & tail_ok
            o, m, l = attend(qs_ref[qrow, :], ks_ref[krow, :], vs_ref[krow, :], ok)
            acc_ref[1, qrow, :] = o
            m_ref[1, qrow, :] = m
            l_ref[1, qrow, :] = l
        for r in range(nc):
            qrow = pl.ds(c * qc + r, TQ, stride=nc)
            krow = pl.ds(c * kc + r, 2 * TQ, stride=nc)
            o, m, l = attend(qs_ref[qrow, :], ks_ref[krow, :], vs_ref[krow, :], band & head_ok & tail_ok)
            acc_ref[2, qrow, :] = o
            m_ref[2, qrow, :] = m
            l_ref[2, qrow, :] = l

    m_all = jnp.maximum(jnp.maximum(m_ref[0], m_ref[1]), m_ref[2])
    num = jnp.zeros((tp, LANES), F32)
    den = jnp.zeros((tp, LANES), F32)
    for g in range(len(DILATIONS)):
        w = jnp.exp(m_ref[g] - m_all)
        num = num + w * acc_ref[g]
        den = den + w * l_ref[g]
    res = num / den
    for c in range(nc):
        o_ref[pl.ds(c, qc, stride=nc), :] = res[c * qc:(c + 1) * qc]


def _dil_attn(q, k, v):
    b, s, _ = q.shape
    tp = TP_ATTN
    n_tiles = s // tp
    cur = lambda bb, h, i: (bb, i, h)
    prv = lambda bb, h, i: (bb, jnp.maximum(i - 1, 0), h)
    nxt = lambda bb, h, i: (bb, jnp.minimum(i + 1, n_tiles - 1), h)
    blk = (None, tp, HEAD_DIM)
    return pl.pallas_call(
        functools.partial(_attn_kernel, n_tiles=n_tiles),
        grid=(b, N_HEADS, n_tiles),
        in_specs=[
            pl.BlockSpec(blk, cur),
            pl.BlockSpec(blk, prv), pl.BlockSpec(blk, cur), pl.BlockSpec(blk, nxt),
            pl.BlockSpec(blk, prv), pl.BlockSpec(blk, cur), pl.BlockSpec(blk, nxt),
        ],
        out_specs=pl.BlockSpec(blk, cur),
        out_shape=jax.ShapeDtypeStruct((b, s, ATTN_WIDTH), F32),
        scratch_shapes=[pltpu.VMEM((tp, LANES), F32),
                        pltpu.VMEM((tp + 2 * HALF_WINDOW * DILATIONS[-1], LANES), F32),
                        pltpu.VMEM((tp + 2 * HALF_WINDOW * DILATIONS[-1], LANES), F32)]
        + [pltpu.VMEM((len(DILATIONS), tp, LANES), F32)] * 3,
        compiler_params=_cparams(("arbitrary", "arbitrary", "arbitrary")),
        name="dil_attn",
    )(q, k, k, k, v, v, v)


def _rglru_kernel(xf_p, xf_c, xf_n, xr_p, xr_c, xr_n, cw_ref, cb_ref, w_ref, ba_ref, bx_ref,
                  lam_ref, hf_ref, hr_ref, xe_ref, xc_s, a_ref, b_ref, h_ref, carry_ref, *, n_tiles):
    i = pl.program_id(1)
    ts = xf_c.shape[0]

    @pl.when(i == 0)
    def _():
        carry_ref[...] = jnp.zeros_like(carry_ref)

    def gates(direction, xp, xc_ref, xn, at_start, at_end):
        nb = N_LRU_BLOCKS
        for g in range(nb):
            sl = slice(g * LRU_BLOCK, (g + 1) * LRU_BLOCK)
            xe_ref[pl.ds(g, SUBLANES, stride=nb), :] = jnp.where(at_start, 0.0, xp[:, sl])
            xe_ref[pl.ds(SUBLANES * nb + g, ts, stride=nb), :] = xc_ref[:, sl]
            xe_ref[pl.ds((SUBLANES + ts) * nb + g, SUBLANES, stride=nb), :] = jnp.where(at_end, 0.0, xn[:, sl])
        base = SUBLANES - CONV_LEFT
        acc = cb_ref[...][None]
        for jj in range(CONV_WIDTH):
            tap = xe_ref[pl.ds((base + jj) * nb, ts * nb), :].reshape(ts, nb, LANES)
            acc = acc + tap * cw_ref[jj][None]
        xc_s[...] = acc.reshape(ts * nb, LANES)
        lam = lam_ref[direction:direction + 1, :]
        z = -lam
        softplus = jnp.maximum(z, 0.0) + jnp.log(1.0 + jnp.exp(-jnp.abs(z)))
        c = -LRU_C * softplus
        for g in range(N_LRU_BLOCKS):
            sl = slice(g * LRU_BLOCK, (g + 1) * LRU_BLOCK)
            xg = xc_s[pl.ds(g, ts, stride=nb), :]
            y = jnp.dot(xg.astype(BF16), w_ref[direction, g], preferred_element_type=F32)
            r = _sigmoid(y[:, :LRU_BLOCK] + ba_ref[direction:direction + 1, sl])
            gi = _sigmoid(y[:, LRU_BLOCK:] + bx_ref[direction:direction + 1, sl])
            log_a = c[:, sl] * r
            a = jnp.exp(log_a)
            bt = jnp.sqrt(-jnp.tanh(log_a) * (a * a + 1.0)) * gi * xg
            a_ref[direction, pl.ds(g, ts, stride=N_LRU_BLOCKS), :] = a
            b_ref[direction, pl.ds(g, ts, stride=N_LRU_BLOCKS), :] = bt

    gates(0, xf_p, xf_c, xf_n, i == 0, i == n_tiles - 1)
    gates(1, xr_p, xr_c, xr_n, i == n_tiles - 1, i == 0)

    unroll = 8

    def step(it, carry):
        hf, hr = carry
        for uu in range(unroll):
            tf = pl.multiple_of((it * unroll + uu) * SUBLANES, SUBLANES)
            tr = pl.multiple_of((ts - 1 - it * unroll - uu) * SUBLANES, SUBLANES)
            hf = a_ref[0, pl.ds(tf, SUBLANES), :] * hf + b_ref[0, pl.ds(tf, SUBLANES), :]
            hr = a_ref[1, pl.ds(tr, SUBLANES), :] * hr + b_ref[1, pl.ds(tr, SUBLANES), :]
            h_ref[0, pl.ds(tf, SUBLANES), :] = hf
            h_ref[1, pl.ds(tr, SUBLANES), :] = hr
        return hf, hr

    hf, hr = lax.fori_loop(0, ts // unroll, step, (carry_ref[0], carry_ref[1]))
    carry_ref[0] = hf
    carry_ref[1] = hr
    for g in range(N_LRU_BLOCKS):
        sl = slice(g * LRU_BLOCK, (g + 1) * LRU_BLOCK)
        hf_ref[:, sl] = h_ref[0, pl.ds(g, ts, stride=N_LRU_BLOCKS), :]
        hr_ref[:, sl] = h_ref[1, pl.ds(g, ts, stride=N_LRU_BLOCKS), :]


def _rglru(xr, conv_w, conv_b, w_cat, ba, bx, lam):
    b, s, _ = xr.shape
    ts = TS_LRU
    n_tiles = s // ts
    per = ts // SUBLANES
    n8 = s // SUBLANES
    halo = (None, SUBLANES, LRU_WIDTH)
    tile = (None, ts, LRU_WIDTH)
    rev = lambda i: n_tiles - 1 - i
    const2 = lambda bb, i: (0, 0)
    in_specs = [
        pl.BlockSpec(halo, lambda bb, i: (bb, jnp.maximum(i * per - 1, 0), 0)),
        pl.BlockSpec(tile, lambda bb, i: (bb, i, 0)),
        pl.BlockSpec(halo, lambda bb, i: (bb, jnp.minimum((i + 1) * per, n8 - 1), 0)),
        pl.BlockSpec(halo, lambda bb, i: (bb, jnp.maximum(rev(i) * per - 1, 0), 0)),
        pl.BlockSpec(tile, lambda bb, i: (bb, rev(i), 0)),
        pl.BlockSpec(halo, lambda bb, i: (bb, jnp.minimum((rev(i) + 1) * per, n8 - 1), 0)),
        pl.BlockSpec((CONV_WIDTH, N_LRU_BLOCKS, LRU_BLOCK), lambda bb, i: (0, 0, 0)),
        pl.BlockSpec((N_LRU_BLOCKS, LRU_BLOCK), const2),
        pl.BlockSpec((2, N_LRU_BLOCKS, LRU_BLOCK, 2 * LRU_BLOCK), lambda bb, i: (0, 0, 0, 0)),
        pl.BlockSpec((2, LRU_WIDTH), const2),
        pl.BlockSpec((2, LRU_WIDTH), const2),
        pl.BlockSpec((2, LRU_WIDTH), const2),
    ]
    out = jax.ShapeDtypeStruct((b, s, LRU_WIDTH), F32)
    return pl.pallas_call(
        functools.partial(_rglru_kernel, n_tiles=n_tiles),
        grid=(b, n_tiles),
        in_specs=in_specs,
        out_specs=[pl.BlockSpec(tile, lambda bb, i: (bb, i, 0)),
                   pl.BlockSpec(tile, lambda bb, i: (bb, rev(i), 0))],
        out_shape=[out, out],
        scratch_shapes=[
            pltpu.VMEM(((ts + 2 * SUBLANES) * N_LRU_BLOCKS, LANES), F32),
            pltpu.VMEM((ts * N_LRU_BLOCKS, LANES), F32),
            pltpu.VMEM((2, ts * N_LRU_BLOCKS, LANES), F32),
            pltpu.VMEM((2, ts * N_LRU_BLOCKS, LANES), F32),
            pltpu.VMEM((2, ts * N_LRU_BLOCKS, LANES), F32),
            pltpu.VMEM((2, SUBLANES, LANES), F32),
        ],
        compiler_params=_cparams(("arbitrary", "arbitrary")),
        name="rglru",
    )(xr, xr, xr, xr, xr, xr, conv_w, conv_b, w_cat, ba, bx, lam)


def _out_proj_kernel(attn_ref, hf_ref, hr_ref, gate_ref, x_ref, ga_ref, gl_ref, w_ref, g2_ref, wq_ref,
                     x1_ref, xn_ref, pq_ref):
    lru = (hf_ref[...] + hr_ref[...]) * _gelu(gate_ref[...])
    na = _rms(attn_ref[...], ga_ref[...]).astype(BF16)
    nl = _rms(lru, gl_ref[...]).astype(BF16)
    mix = jnp.dot(jnp.concatenate([na, nl], axis=-1), w_ref[...], preferred_element_type=F32)
    x1 = x_ref[...] + mix
    x1_ref[...] = x1
    xn = _rms(x1, g2_ref[...]).astype(BF16)
    xn_ref[...] = xn
    pq_ref[...] = jnp.dot(xn, wq_ref[...], preferred_element_type=F32).astype(BF16)


def _out_proj(attn, hf, hr, gate, x, ga, gl, w_out, g2, wq):
    t = x.shape[0]
    tm = TM_OUT
    tok = lambda i: (i, 0)
    const = lambda i: (0, 0)
    half = pl.BlockSpec((tm, ATTN_WIDTH), tok)
    full = pl.BlockSpec((tm, D_MODEL), tok)
    weight = pl.BlockSpec((D_MODEL, D_MODEL), const, pipeline_mode=pl.Buffered(1))
    act = jax.ShapeDtypeStruct((t, D_MODEL), BF16)
    return pl.pallas_call(
        _out_proj_kernel,
        grid=(t // tm,),
        in_specs=[half, half, half, half, full,
                  pl.BlockSpec((1, ATTN_WIDTH), const), pl.BlockSpec((1, LRU_WIDTH), const),
                  weight, pl.BlockSpec((1, D_MODEL), const), weight],
        out_specs=[full, full, full],
        out_shape=[jax.ShapeDtypeStruct((t, D_MODEL), F32), act, act],
        compiler_params=_cparams(("arbitrary",)),
        name="out_proj",
    )(attn, hf, hr, gate, x, ga, gl, w_out, g2, wq)


def _top16_rows(x, codes):
    big = jnp.iinfo(jnp.int32).max
    vals, ids = [], []
    for _ in range(PEER_TOPK):
        m = jnp.max(x, axis=0, keepdims=True)
        idx = jnp.min(jnp.where(x == m, codes, big), axis=0, keepdims=True)
        x = jnp.where(codes == idx, -jnp.inf, x)
        vals.append(m)
        ids.append(idx)
    return jnp.concatenate(vals, axis=0), jnp.concatenate(ids, axis=0)


def _sort16_network():
    def merge(lo, hi, r):
        step = r * 2
        if step < hi - lo:
            yield from merge(lo, hi, step)
            yield from merge(lo + r, hi, step)
            for i in range(lo + r, hi - r, step):
                yield (i, i + r)
        else:
            yield (lo, lo + r)

    def sort(lo, hi):
        if hi - lo >= 1:
            mid = lo + (hi - lo) // 2
            yield from sort(lo, mid)
            yield from sort(mid + 1, hi)
            yield from merge(lo, hi, 1)

    return tuple(sort(0, PEER_TOPK - 1))


SORT16 = _sort16_network()


def _top16_of_keys(sc):
    lanes = sc.shape[1]
    sub = lax.broadcasted_iota(I32, (SUBLANES, lanes), 0)
    v = [sc[SUBLANES * j:SUBLANES * (j + 1), :] for j in range(PEER_TOPK)]
    ix = [sub + SUBLANES * j for j in range(PEER_TOPK)]
    for i, j in SORT16:
        keep = v[i] >= v[j]
        v[i], v[j] = jnp.maximum(v[i], v[j]), jnp.minimum(v[i], v[j])
        ix[i], ix[j] = jnp.where(keep, ix[i], ix[j]), jnp.where(keep, ix[j], ix[i])
    big = jnp.iinfo(jnp.int32).max
    vals, ids = [], []
    for a in range(PEER_TOPK):
        m = jnp.max(v[0], axis=0, keepdims=True)
        idx = jnp.min(jnp.where(v[0] == m, ix[0], big), axis=0, keepdims=True)
        vals.append(m)
        ids.append(idx)
        win = ix[0] == idx
        for j in range(PEER_TOPK - 1 - a):
            v[j] = jnp.where(win, v[j + 1], v[j])
            ix[j] = jnp.where(win, ix[j + 1], ix[j])
    return jnp.concatenate(vals, axis=0), jnp.concatenate(ids, axis=0)


def _route_tile(s1, s2, cand_codes):
    v1, id1 = _top16_of_keys(s1)
    v2, id2 = _top16_of_keys(s2)
    cand = jnp.concatenate(
        [v1[0:1] + v2] + [v1[a:a + 1] + v2[0:8] for a in range(1, 8)] + [v1[8:16] + v2[0:1]], axis=0)
    top, code = _top16_rows(cand, cand_codes)
    ra = code >> 4
    rb = code & (PEER_TOPK - 1)
    i1 = jnp.zeros(ra.shape, I32)
    i2 = jnp.zeros(ra.shape, I32)
    for a in range(PEER_TOPK):
        i1 = jnp.where(ra == a, id1[a:a + 1], i1)
        i2 = jnp.where(rb == a, id2[a:a + 1], i2)
    ex = jnp.exp(top - top[0:1])
    return i1, i2, ex / jnp.sum(ex, axis=0, keepdims=True)


def _peer_kernel(xn_ref, x1_ref, pq_ref, keys_ref, ut_ref, v_ref, y_ref,
                 coef_ref, rt1_ref, rt2_ref, rtg_ref, tk1_ref, tk2_ref, tkg_ref):
    i = pl.program_id(0)
    e = pl.program_id(1)
    tm = xn_ref.shape[0]
    n_q = tm // LANES
    n_blocks = N_EXPERTS // EB_PEER
    i1_per_block = EB_PEER // N_KEYS
    tiles_per_step = pq_ref.shape[0] // LANES
    steps_per_head = n_q // tiles_per_step
    high_half = -(1 << 16)

    def route_part():
        h = e // steps_per_head
        q0 = (e % steps_per_head) * tiles_per_step
        row16 = lax.broadcasted_iota(I32, (PEER_TOPK, LANES), 0)
        row8 = lax.broadcasted_iota(I32, (SUBLANES, LANES), 0)
        cand_codes = jnp.concatenate(
            [row16] + [a * PEER_TOPK + row8 for a in range(1, 8)] + [(row8 + 8) * PEER_TOPK], axis=0)
        nt = (((1,), (1,)), ((), ()))
        rows = pl.ds(pl.multiple_of(h * PEER_TOPK, PEER_TOPK), PEER_TOPK)
        for k in range(tiles_per_step):
            pq = pq_ref[k * LANES:(k + 1) * LANES, :]
            s1 = lax.dot_general(keys_ref[2 * h], pq[:, :PEER_HALF], nt, preferred_element_type=F32)
            s2 = lax.dot_general(keys_ref[2 * h + 1], pq[:, PEER_HALF:], nt, preferred_element_type=F32)
            i1, i2, g = _route_tile(s1, s2, cand_codes)
            rt1_ref[q0 + k, rows, :] = i1
            rt2_ref[q0 + k, rows, :] = i2
            rtg_ref[q0 + k, rows, :] = g

    def dense_part():
        local = (e % (n_blocks // 2)) * i1_per_block
        shift = jnp.where(e < n_blocks // 2, 16, 0)
        i1_per_pass = i1_per_block // DENSE_PASSES
        width = i1_per_pass * N_KEYS
        for p in range(DENSE_PASSES):
            cols = slice(p * width, (p + 1) * width)
            act = _gelu(jnp.dot(xn_ref[...], ut_ref[:, cols], preferred_element_type=F32))
            parts = []
            for jj in range(i1_per_pass):
                word = coef_ref[pl.ds(local + p * i1_per_pass + jj, tm, stride=COEF_PITCH), :]
                cf = lax.bitcast_convert_type(jnp.left_shift(word, shift) & high_half, F32)
                parts.append((cf * act[:, jj * N_KEYS:(jj + 1) * N_KEYS]).astype(BF16))
            y_ref[...] += jnp.dot(jnp.concatenate(parts, axis=-1), v_ref[cols, :], preferred_element_type=F32)

    @pl.when(i == 0)
    def _():
        route_part()

    @pl.when(i > 0)
    def _():
        @pl.when(e == 0)
        def _():
            y_ref[...] = x1_ref[...]
            for q in range(n_q):
                tok = slice(q * LANES, (q + 1) * LANES)
                tk1_ref[tok, :] = rt1_ref[q].T
                tk2_ref[tok, :] = rt2_ref[q].T
                tkg_ref[tok, :] = rtg_ref[q].T

            rows = lax.broadcasted_iota(I32, (N_KEYS, N_TERMS), 0)

            def token_group(gi, carry):
                t0 = pl.multiple_of(gi * COEF_GROUP, COEF_GROUP)
                i1b = tk1_ref[pl.ds(t0, COEF_GROUP), :]
                i2b = tk2_ref[pl.ds(t0, COEF_GROUP), :]
                gb = tkg_ref[pl.ds(t0, COEF_GROUP), :]
                zero = jnp.zeros((N_KEYS, N_TERMS), BF16)
                for uu in range(0, COEF_GROUP, 2):
                    a1, a2 = [], []
                    for v in (uu, uu + 1):
                        a1.append(jnp.where(rows == i1b[v:v + 1, :], gb[v:v + 1, :], 0.0).astype(BF16))
                        a2.append(jnp.where(rows == i2b[v:v + 1, :], 1.0, 0.0).astype(BF16))
                    lhs = jnp.concatenate([jnp.concatenate([a1[0], zero], axis=1),
                                           jnp.concatenate([zero, a1[1]], axis=1)], axis=0)
                    rhs = jnp.concatenate(a2, axis=1)
                    c = lax.dot_general(lhs, rhs, (((1,), (1,)), ((), ())), preferred_element_type=F32)
                    for k, v in enumerate((uu, uu + 1)):
                        lo = c[k * N_KEYS:k * N_KEYS + COEF_ROWS].astype(BF16).astype(F32)
                        hi = c[k * N_KEYS + COEF_ROWS:(k + 1) * N_KEYS].astype(BF16).astype(F32)
                        word = (lax.bitcast_convert_type(hi, I32)
                                | lax.shift_right_logical(lax.bitcast_convert_type(lo, I32), 16))
                        row = pl.multiple_of((t0 + v) * COEF_PITCH, SUBLANES)
                        coef_ref[pl.ds(row, COEF_ROWS), :] = word
                return carry

            lax.fori_loop(0, tm // COEF_GROUP, token_group, 0)

        route_part()
        dense_part()


def _peer(xn, x1, pq, keys, ut, v):
    t = xn.shape[0]
    tm = TM_PEER
    n_tiles = t // tm
    n_q = tm // LANES
    n_blocks = N_EXPERTS // EB_PEER
    steps_per_head = n_blocks // PEER_HEADS
    tiles_per_step = n_q // steps_per_head
    assert steps_per_head * PEER_HEADS == n_blocks and tiles_per_step * steps_per_head == n_q
    prev = lambda i, e: (jnp.maximum(i - 1, 0), 0)
    blk = lambda i, e: jnp.where(i == 0, 0, e)
    return pl.pallas_call(
        _peer_kernel,
        grid=(n_tiles + 1, n_blocks),
        in_specs=[pl.BlockSpec((tm, D_MODEL), prev),
                  pl.BlockSpec((tm, D_MODEL), prev),
                  pl.BlockSpec((tiles_per_step * LANES, 2 * PEER_HALF),
                               lambda i, e: (jnp.minimum(i, n_tiles - 1) * steps_per_head + e % steps_per_head,
                                             e // steps_per_head)),
                  pl.BlockSpec((2 * PEER_HEADS, N_KEYS, PEER_HALF), lambda i, e: (0, 0, 0)),
                  pl.BlockSpec((D_MODEL, EB_PEER), lambda i, e: (0, blk(i, e))),
                  pl.BlockSpec((EB_PEER, D_MODEL), lambda i, e: (blk(i, e), 0))],
        out_specs=pl.BlockSpec((tm, D_MODEL), prev),
        out_shape=jax.ShapeDtypeStruct((t, D_MODEL), F32),
        scratch_shapes=[pltpu.VMEM((tm * COEF_PITCH, LANES), I32),
                        pltpu.VMEM((n_q, N_TERMS, LANES), I32),
                        pltpu.VMEM((n_q, N_TERMS, LANES), I32),
                        pltpu.VMEM((n_q, N_TERMS, LANES), F32),
                        pltpu.VMEM((tm, N_TERMS), I32),
                        pltpu.VMEM((tm, N_TERMS), I32),
                        pltpu.VMEM((tm, N_TERMS), F32)],
        compiler_params=_cparams(("arbitrary", "arbitrary")),
        name="peer",
    )(xn, x1, pq, keys, ut, v)


def _rope_tables(seq):
    inv = ROPE_THETA ** (-jnp.arange(0, HEAD_DIM, 2, dtype=F32) / HEAD_DIM)
    ang = jnp.arange(seq, dtype=F32)[:, None] * inv[None, :]
    cos, sin = jnp.cos(ang), jnp.sin(ang)
    return jnp.concatenate([cos, cos], axis=-1), jnp.concatenate([-sin, sin], axis=-1)


def _cast_transpose_kernel(x_ref, o_ref):
    o_ref[...] = x_ref[...].T.astype(BF16)


def _cast_transpose(stacked, layer):
    _, rows, cols = stacked.shape
    return pl.pallas_call(
        _cast_transpose_kernel,
        grid=(rows // EB_PEER,),
        in_specs=[pl.BlockSpec((None, EB_PEER, cols), lambda e: (layer, e, 0))],
        out_specs=pl.BlockSpec((cols, EB_PEER), lambda e: (0, e)),
        out_shape=jax.ShapeDtypeStruct((cols, rows), BF16),
        compiler_params=_cparams(("arbitrary",)),
        name="cast_transpose",
    )(stacked)


def _prep_layer(params, layer):
    (ln1_g, w_in, qg, kg, conv_w, conv_b, wa, ba, wx, bx, lam, ga, gl, w_out, g2, wq, keys, _, v) = (
        p[layer] for p in params)
    u_stacked = params[-2]
    w_cat = jnp.concatenate([wa, wx], axis=-1).astype(BF16)
    return dict(
        ln1_g=ln1_g.reshape(1, -1), w_in=w_in.astype(BF16), qg=qg.reshape(1, -1), kg=kg.reshape(1, -1),
        conv_w=conv_w.reshape(CONV_WIDTH, N_LRU_BLOCKS, LRU_BLOCK), conv_b=conv_b.reshape(N_LRU_BLOCKS, LRU_BLOCK),
        w_cat=w_cat, ba=ba, bx=bx, lam=lam,
        ga=ga.reshape(1, -1), gl=gl.reshape(1, -1), w_out=w_out.astype(BF16), g2=g2.reshape(1, -1),
        wq=wq.astype(BF16), keys=keys.reshape(2 * PEER_HEADS, N_KEYS, PEER_HALF).astype(BF16),
        ut=_cast_transpose(u_stacked, layer), v=v.astype(BF16))


def _layer(x, w, cos, sin):
    b, s, _ = x.shape
    xt = x.reshape(b * s, D_MODEL)
    q, k, v, xr, gate = _in_proj(xt, w["ln1_g"], w["w_in"], w["qg"], w["kg"], cos, sin, s)
    shp = (b, s, ATTN_WIDTH)
    attn = _dil_attn(q.reshape(shp), k.reshape(shp), v.reshape(shp))
    hf, hr = _rglru(xr.reshape(shp), w["conv_w"], w["conv_b"], w["w_cat"], w["ba"], w["bx"], w["lam"])
    x1, xn, pq = _out_proj(attn.reshape(b * s, -1), hf.reshape(b * s, -1), hr.reshape(b * s, -1), gate, xt,
                           w["ga"], w["gl"], w["w_out"], w["g2"], w["wq"])
    y = _peer(xn, x1, pq, w["keys"], w["ut"], w["v"])
    return y.reshape(b, s, D_MODEL)


def kernel(x_prompt, x_sample, ln1_g, w_in, q_norm_g, k_norm_g, conv_w, conv_b, lru_wa, lru_ba, lru_wx, lru_bx, lru_lam, attn_out_g, lru_out_g, w_out, ln2_g, peer_wq, peer_keys, peer_u, peer_v):
    params = (ln1_g, w_in, q_norm_g, k_norm_g, conv_w, conv_b, lru_wa, lru_ba, lru_wx, lru_bx,
              lru_lam, attn_out_g, lru_out_g, w_out, ln2_g, peer_wq, peer_keys, peer_u, peer_v)
    depth = ln1_g.shape[0]
    layers = [_prep_layer(params, l) for l in range(depth)]

    def trunk(x):
        cos, sin = _rope_tables(x.shape[1])
        for w in layers:
            x = _layer(x, w, cos, sin)
        return x

    return (trunk(x_prompt), trunk(x_sample))
```
